```python
import math, functools
import jax, jax.numpy as jnp
from jax import lax
import numpy as np

D_MODEL = 1024
BATCH = 4
SEQ = 4096
DEPTH = 2
DEC_BATCH = 32
DEC_SEQ = 4
PAST_LEN = 16384
PAGE_SIZE = 128

N_HEADS = 8
HEAD_DIM = 64
V_DIM = 2 * HEAD_DIM
QK_DIM = N_HEADS * 2 * HEAD_DIM
ATTN_DIM = N_HEADS * V_DIM
SSM_DIM = D_MODEL // 2
SSM_GROUP = 16
SSM_GROUPS = SSM_DIM // SSM_GROUP
SSM_STATE = 64
CONV_DIM = D_MODEL // 2
CONV_W = 3
FFN_DIM = -(-8 * D_MODEL // (3 * 256)) * 256
PLE_DIM = 256
REL_BUCKETS = 32
REL_MAX_DIST = 128
Q_BLOCK = 128
N_BRANCH = 3
IN_DIM = 2 * QK_DIM + ATTN_DIM + SSM_DIM + 3 * CONV_DIM + N_BRANCH * D_MODEL
EPS = 1e-6
NEG = -1e30

kernel_name = "hybrid_diffattn_s5_shortconv_decode_step"


def rmsnorm(x, g):
    xf = x.astype(jnp.float32)
    y = xf * lax.rsqrt(jnp.mean(xf * xf, axis=-1, keepdims=True) + EPS)
    return (y * g.astype(jnp.float32)).astype(x.dtype)


def rel_bucket(n):
    n = jnp.maximum(n, 0)
    max_exact = REL_BUCKETS // 2
    nf = jnp.maximum(n, max_exact).astype(jnp.float32)
    large = max_exact + (jnp.log(nf / max_exact) / math.log(REL_MAX_DIST / max_exact)
                         * (REL_BUCKETS - max_exact)).astype(jnp.int32)
    large = jnp.minimum(large, REL_BUCKETS - 1)
    return jnp.where(n < max_exact, n, large)


def diff_logits(q, q_pos, k, k_pos, rel_table):
    s = jnp.einsum('bthmd,bshmd->bhmts', q, k, preferred_element_type=jnp.float32) * (HEAD_DIM ** -0.5)
    n = q_pos[:, None] - k_pos[None, :]
    bias = rel_table.astype(jnp.float32)[rel_bucket(n)]
    s = s + jnp.transpose(bias, (2, 0, 1))[None, :, None]
    return jnp.where((n >= 0)[None, None, None], s, NEG)


def diff_combine(logits, vs, lam):
    p = jax.nn.softmax(jnp.concatenate(logits, axis=-1), axis=-1)
    a = p[:, :, 0] - lam * p[:, :, 1]
    out = None
    start = 0
    for v in vs:
        n = v.shape[1]
        o = jnp.einsum('bhts,bshe->bthe', a[..., start:start + n].astype(v.dtype), v,
                       preferred_element_type=jnp.float32)
        out = o if out is None else out + o
        start += n
    return out


def prompt_attend(q, k, v, lam, rel_table):
    b, s = q.shape[:2]
    nb = s // Q_BLOCK
    pos = jnp.arange(s, dtype=jnp.int32)
    qb = jnp.moveaxis(q.reshape(b, nb, Q_BLOCK, N_HEADS, 2, HEAD_DIM), 1, 0)

    def block(args):
        i, qi = args
        q_pos = i * Q_BLOCK + jnp.arange(Q_BLOCK, dtype=jnp.int32)
        return diff_combine([diff_logits(qi, q_pos, k, pos, rel_table)], [v], lam)

    o = lax.map(block, (jnp.arange(nb, dtype=jnp.int32), qb))
    return jnp.moveaxis(o, 0, 1).reshape(b, s, N_HEADS, V_DIM)


def sample_attend(q, k, v, lam, rel_table, cache_k, cache_v, page_table, layer):
    db, t = q.shape[:2]
    past = page_table.shape[1] * PAGE_SIZE
    k_past = cache_k[layer, page_table].reshape(db, past, N_HEADS, 2, HEAD_DIM)
    v_past = cache_v[layer, page_table].reshape(db, past, N_HEADS, V_DIM)
    q_pos = past + jnp.arange(t, dtype=jnp.int32)
    k_pos = jnp.arange(past, dtype=jnp.int32)
    logits_past = diff_logits(q, q_pos, k_past, k_pos, rel_table)
    logits_new = diff_logits(q, q_pos, k, q_pos, rel_table)
    return diff_combine([logits_past, logits_new], [v_past, v], lam)


def ssm_branch(u, h0_re, h0_im, a_re, a_im, log_dt, b_re, b_im, c_re, c_im, d_skip):
    f32 = jnp.float32
    bsz, t, _ = u.shape
    uf = u.astype(f32).reshape(bsz, t, SSM_GROUPS, SSM_GROUP)
    lam = lax.complex(a_re.astype(f32), a_im.astype(f32))
    dt = jnp.exp(log_dt.astype(f32))[:, None]
    lam_bar = jnp.exp(lam * dt)
    b_mat = lax.complex(b_re.astype(f32), b_im.astype(f32))
    c_mat = lax.complex(c_re.astype(f32), c_im.astype(f32))
    b_bar = ((lam_bar - 1.0) / lam)[..., None] * b_mat
    bu = jnp.einsum('gpc,btgc->btgp', b_bar, uf.astype(jnp.complex64))
    a_seq = jnp.broadcast_to(lam_bar, bu.shape)

    def combine(e1, e2):
        a1, b1 = e1
        a2, b2 = e2
        return a1 * a2, a2 * b1 + b2

    a_cum, b_cum = lax.associative_scan(combine, (a_seq, bu), axis=1)
    h0 = lax.complex(h0_re.astype(f32), h0_im.astype(f32))
    h = a_cum * h0[:, None] + b_cum
    y = jnp.einsum('gcp,btgp->btgc', c_mat, h).real + d_skip.astype(f32).reshape(SSM_GROUPS, SSM_GROUP) * uf
    return y.reshape(bsz, t, SSM_DIM), h[:, -1].real, h[:, -1].imag


def short_conv(z, buf, w):
    t = z.shape[1]
    zf = jnp.concatenate([buf.astype(z.dtype), z], axis=1)
    y = w[0] * zf[:, 0:t]
    for j in range(1, CONV_W):
        y = y + w[j] * zf[:, j:j + t]
    return y, zf[:, t:]


def trunk_layer(x, pe, attend, h0_re, h0_im, conv_buf, lambda_init,
                g_mix, w_in, q_norm, k_norm, lq1, lk1, lq2, lk2, sub_norm, w_attn_out,
                a_re, a_im, log_dt, b_re, b_im, c_re, c_im, d_skip, w_glu_a, w_glu_b,
                w_conv, w_conv_out, w_o, g_ffn, w_ffn_gate, w_ffn_up, w_ffn_down,
                g_ple, w_ple_gate, w_ple_proj):
    f32 = jnp.float32
    b, t, _ = x.shape
    h = rmsnorm(x, g_mix)
    proj = h @ w_in
    cuts = np.cumsum([QK_DIM, QK_DIM, ATTN_DIM, SSM_DIM, CONV_DIM, CONV_DIM, CONV_DIM]).tolist()
    q, k, v, u, cb, cc, cx, gates = jnp.split(proj, cuts, axis=-1)

    q = rmsnorm(q.reshape(b, t, N_HEADS, 2, HEAD_DIM), q_norm)
    k = rmsnorm(k.reshape(b, t, N_HEADS, 2, HEAD_DIM), k_norm)
    v = v.reshape(b, t, N_HEADS, V_DIM)
    lam = (jnp.exp(jnp.sum(lq1.astype(f32) * lk1.astype(f32)))
           - jnp.exp(jnp.sum(lq2.astype(f32) * lk2.astype(f32))) + lambda_init)
    o = attend(q, k, v, lam)
    o = rmsnorm(o, sub_norm) * (1.0 - lambda_init)
    attn_out = o.reshape(b, t, ATTN_DIM).astype(x.dtype) @ w_attn_out

    y_s, s_re, s_im = ssm_branch(u, h0_re, h0_im, a_re, a_im, log_dt, b_re, b_im, c_re, c_im, d_skip)
    y_s = jax.nn.gelu(y_s).astype(x.dtype)
    ssm_out = (y_s @ w_glu_a) * jax.nn.sigmoid(y_s @ w_glu_b)

    zc, conv_new = short_conv(cc * cx, conv_buf, w_conv)
    conv_out = (cb * zc) @ w_conv_out

    g = jax.nn.sigmoid(gates.reshape(b, t, N_BRANCH, D_MODEL))
    merged = g[:, :, 0] * attn_out + g[:, :, 1] * ssm_out + g[:, :, 2] * conv_out
    x = x + merged @ w_o

    h2 = rmsnorm(x, g_ffn)
    x = x + (jax.nn.silu(h2 @ w_ffn_gate) * (h2 @ w_ffn_up)) @ w_ffn_down

    x = x + jax.nn.sigmoid(rmsnorm(x, g_ple) @ w_ple_gate) * (pe @ w_ple_proj)
    return (x, k.reshape(b, t, N_HEADS, 2 * HEAD_DIM), v, s_re, s_im, conv_new)


def setup_inputs(seed: int = 0) -> dict:
    key = jax.random.key(seed)
    ks = iter(jax.random.split(key, 64))
    f32 = jnp.float32

    def nrm(shape, scale=1.0):
        return jax.random.normal(next(ks), shape, f32) * scale

    def gain(shape):
        return 1.0 + nrm(shape, 0.05)

    n_pages = PAST_LEN // PAGE_SIZE
    n_pool = (DEC_BATCH * n_pages * 5) // 4
    page_table = jax.random.permutation(next(ks), n_pool)[:DEC_BATCH * n_pages]
    page_table = page_table.reshape(DEC_BATCH, n_pages).astype(jnp.int32)
    n_idx = jnp.arange(SSM_STATE, dtype=f32)
    return {
        "x_prompt": nrm((BATCH, SEQ, D_MODEL)),
        "x_sample": nrm((DEC_BATCH, DEC_SEQ, D_MODEL)),
        "cache_k": nrm((DEPTH, n_pool, PAGE_SIZE, N_HEADS, 2 * HEAD_DIM)),
        "cache_v": nrm((DEPTH, n_pool, PAGE_SIZE, N_HEADS, V_DIM)),
        "state_ssm_re": nrm((DEPTH, DEC_BATCH, SSM_GROUPS, SSM_STATE), 0.5),
        "state_ssm_im": nrm((DEPTH, DEC_BATCH, SSM_GROUPS, SSM_STATE), 0.5),
        "state_conv": nrm((DEPTH, DEC_BATCH, CONV_W - 1, CONV_DIM)),
        "page_table": page_table,
        "p_prompt": nrm((DEPTH, BATCH, SEQ, PLE_DIM)),
        "p_sample": nrm((DEPTH, DEC_BATCH, DEC_SEQ, PLE_DIM)),
        "rel_table": nrm((REL_BUCKETS, N_HEADS), 0.5),
        "g_mix": gain((DEPTH, D_MODEL)),
        "w_in": nrm((DEPTH, D_MODEL, IN_DIM), D_MODEL ** -0.5),
        "q_norm": gain((DEPTH, HEAD_DIM)),
        "k_norm": gain((DEPTH, HEAD_DIM)),
        "lambda_q1": nrm((DEPTH, HEAD_DIM), 0.1),
        "lambda_k1": nrm((DEPTH, HEAD_DIM), 0.1),
        "lambda_q2": nrm((DEPTH, HEAD_DIM), 0.1),
        "lambda_k2": nrm((DEPTH, HEAD_DIM), 0.1),
        "sub_norm": gain((DEPTH, V_DIM)),
        "w_attn_out": nrm((DEPTH, ATTN_DIM, D_MODEL), ATTN_DIM ** -0.5),
        "ssm_a_re": -0.5 + nrm((DEPTH, SSM_GROUPS, SSM_STATE), 0.01),
        "ssm_a_im": math.pi * n_idx + nrm((DEPTH, SSM_GROUPS, SSM_STATE), 0.01),
        "ssm_log_dt": jax.random.uniform(next(ks), (DEPTH, SSM_GROUPS), f32, math.log(1e-3), math.log(1e-1)),
        "ssm_b_re": nrm((DEPTH, SSM_GROUPS, SSM_STATE, SSM_GROUP), (2 * SSM_GROUP) ** -0.5),
        "ssm_b_im": nrm((DEPTH, SSM_GROUPS, SSM_STATE, SSM_GROUP), (2 * SSM_GROUP) ** -0.5),
        "ssm_c_re": nrm((DEPTH, SSM_GROUPS, SSM_GROUP, SSM_STATE), (2 * SSM_STATE) ** -0.5),
        "ssm_c_im": nrm((DEPTH, SSM_GROUPS, SSM_GROUP, SSM_STATE), (2 * SSM_STATE) ** -0.5),
        "ssm_d": nrm((DEPTH, SSM_DIM)),
        "w_glu_a": nrm((DEPTH, SSM_DIM, D_MODEL), SSM_DIM ** -0.5),
        "w_glu_b": nrm((DEPTH, SSM_DIM, D_MODEL), SSM_DIM ** -0.5),
        "w_conv": nrm((DEPTH, CONV_W, CONV_DIM), CONV_W ** -0.5),
        "w_conv_out": nrm((DEPTH, CONV_DIM, D_MODEL), CONV_DIM ** -0.5),
        "w_o": nrm((DEPTH, D_MODEL, D_MODEL), D_MODEL ** -0.5),
        "g_ffn": gain((DEPTH, D_MODEL)),
        "w_ffn_gate": nrm((DEPTH, D_MODEL, FFN_DIM), D_MODEL ** -0.5),
        "w_ffn_up": nrm((DEPTH, D_MODEL, FFN_DIM), D_MODEL ** -0.5),
        "w_ffn_down": nrm((DEPTH, FFN_DIM, D_MODEL), FFN_DIM ** -0.5),
        "g_ple": gain((DEPTH, D_MODEL)),
        "w_ple_gate": nrm((DEPTH, D_MODEL, D_MODEL), D_MODEL ** -0.5),
        "w_ple_proj": nrm((DEPTH, PLE_DIM, D_MODEL), PLE_DIM ** -0.5),
    }


def reference(x_prompt, x_sample, cache_k, cache_v, state_ssm_re, state_ssm_im, state_conv,
              page_table, p_prompt, p_sample, rel_table, g_mix, w_in, q_norm, k_norm,
              lambda_q1, lambda_k1, lambda_q2, lambda_k2, sub_norm, w_attn_out,
              ssm_a_re, ssm_a_im, ssm_log_dt, ssm_b_re, ssm_b_im, ssm_c_re, ssm_c_im, ssm_d,
              w_glu_a, w_glu_b, w_conv, w_conv_out, w_o, g_ffn, w_ffn_gate, w_ffn_up,
              w_ffn_down, g_ple, w_ple_gate, w_ple_proj):
    xp, xs = x_prompt, x_sample
    bp = x_prompt.shape[0]
    kp_l, vp_l, srp_l, sip_l, cvp_l = [], [], [], [], []
    ks_l, vs_l, srs_l, sis_l, cvs_l = [], [], [], [], []
    prompt_att = functools.partial(prompt_attend, rel_table=rel_table)
    for i in range(DEPTH):
        lw = [w[i] for w in (g_mix, w_in, q_norm, k_norm, lambda_q1, lambda_k1, lambda_q2, lambda_k2,
                             sub_norm, w_attn_out, ssm_a_re, ssm_a_im, ssm_log_dt, ssm_b_re, ssm_b_im,
                             ssm_c_re, ssm_c_im, ssm_d, w_glu_a, w_glu_b, w_conv, w_conv_out, w_o,
                             g_ffn, w_ffn_gate, w_ffn_up, w_ffn_down, g_ple, w_ple_gate, w_ple_proj)]
        lambda_init = 0.8 - 0.6 * math.exp(-0.3 * i)
        zeros_state = jnp.zeros((bp, SSM_GROUPS, SSM_STATE), jnp.float32)
        xp, kp, vp, srp, sip, cvp = trunk_layer(
            xp, p_prompt[i], prompt_att, zeros_state, zeros_state,
            jnp.zeros((bp, CONV_W - 1, CONV_DIM), xp.dtype), lambda_init, *lw)
        sample_att = functools.partial(sample_attend, rel_table=rel_table, cache_k=cache_k,
                                       cache_v=cache_v, page_table=page_table, layer=i)
        xs, kS, vS, srs, sis, cvs = trunk_layer(
            xs, p_sample[i], sample_att, state_ssm_re[i], state_ssm_im[i], state_conv[i],
            lambda_init, *lw)
        kp_l.append(kp); vp_l.append(vp); srp_l.append(srp); sip_l.append(sip); cvp_l.append(cvp)
        ks_l.append(kS); vs_l.append(vS); srs_l.append(srs); sis_l.append(sis); cvs_l.append(cvs)
    k_prompt, v_prompt = jnp.stack(kp_l), jnp.stack(vp_l)
    ssm_re_prompt, ssm_im_prompt, conv_prompt = jnp.stack(srp_l), jnp.stack(sip_l), jnp.stack(cvp_l)
    k_sample, v_sample = jnp.stack(ks_l), jnp.stack(vs_l)
    ssm_re_sample, ssm_im_sample, conv_sample = jnp.stack(srs_l), jnp.stack(sis_l), jnp.stack(cvs_l)
    return (xp, xs, k_prompt, v_prompt, ssm_re_prompt, ssm_im_prompt, conv_prompt,
            k_sample, v_sample, ssm_re_sample, ssm_im_sample, conv_sample)
```

```python
import functools
import math

import jax
import jax.numpy as jnp
from jax import lax
from jax.experimental import pallas as pl
from jax.experimental.pallas import tpu as pltpu

F32 = jnp.float32
BF16 = jnp.bfloat16

N_HEADS = 8
HEAD_DIM = 64
V_DIM = 2 * HEAD_DIM
SSM_GROUP = 16
SSM_STATE = 64
CONV_W = 3
N_BRANCH = 3
REL_BUCKETS = 32
REL_MAX_DIST = 128
EPS = 1e-6
NEG = -1e30

V7X_VMEM_BYTES = 64 * 1024 * 1024
VMEM_LIMIT_BYTES = (V7X_VMEM_BYTES * 7) // 8
MXU_DIM = 256
SUBLANES = 8
ROW_TILE = 256
ATTN_TILE = 256
SSM_CHUNK_ROWS = 256
PAGES_PER_STEP = 8


def _cparams(*sem):
    return pltpu.CompilerParams(dimension_semantics=sem, vmem_limit_bytes=VMEM_LIMIT_BYTES)


def _resident():
    return pl.BlockSpec(memory_space=pltpu.VMEM)


def _rms(x, g):
    ms = jnp.mean(x * x, axis=-1, keepdims=True)
    return x * lax.rsqrt(ms + EPS) * g


def _dot(a, b):
    return jnp.dot(a, b, preferred_element_type=F32)


def _head_norm(acc, seg, gain):
    outs = []
    for c in range(acc.shape[1] // MXU_DIM):
        a = acc[:, c * MXU_DIM:(c + 1) * MXU_DIM]
        ms = _dot((a * a).astype(BF16), seg)
        outs.append(a * lax.rsqrt(ms + EPS))
    return jnp.concatenate(outs, axis=1) * gain


def _in_proj_kernel(*refs, tm, seq_len, d_model, conv_dim, carry_mode):
    if carry_mode:
        (x_ref, gmix_ref, w_ref, qn_ref, kn_ref, seg_ref, wconv_ref, buf_ref,
         q_ref, kf_ref, kb_ref, vf_ref, vb_ref, u_ref, z_ref, cbz_ref, g_ref, carry_ref) = refs
    else:
        (x_ref, gmix_ref, w_ref, qn_ref, kn_ref, seg_ref, wconv_ref, p1_ref, p2_ref,
         q_ref, kf_ref, kb_ref, vf_ref, vb_ref, u_ref, z_ref, cbz_ref, g_ref) = refs
    d = d_model
    h = _rms(x_ref[...], gmix_ref[...]).astype(BF16)

    def proj(c0, c1):
        return _dot(h, w_ref[:, c0:c1])

    seg = seg_ref[...]
    q_ref[...] = _head_norm(proj(0, d), seg, qn_ref[...]).astype(BF16)
    k = _head_norm(proj(d, 2 * d), seg, kn_ref[...])
    kf_ref[...] = k
    kb_ref[...] = k.astype(BF16)
    v = proj(2 * d, 3 * d)
    vf_ref[...] = v
    vb_ref[...] = v.astype(BF16)
    c0 = 3 * d
    ucb = proj(c0, c0 + 2 * conv_dim)
    u_ref[...] = ucb[:, :conv_dim]
    cb = ucb[:, conv_dim:]
    ccx = proj(c0 + 2 * conv_dim, c0 + 4 * conv_dim)
    z = ccx[:, :conv_dim] * ccx[:, conv_dim:]
    z_ref[...] = z

    row = lax.broadcasted_iota(jnp.int32, z.shape, 0)
    zr1 = pltpu.roll(z, 1, axis=0)
    zr2 = pltpu.roll(z, 2, axis=0)
    if carry_mode:
        tiles_per_seq = seq_len // tm

        @pl.when(pl.program_id(0) % tiles_per_seq == 0)
        def _():
            carry_ref[...] = buf_ref[...]

        c_m2 = carry_ref[SUBLANES - 2:SUBLANES - 1, :]
        c_m1 = carry_ref[SUBLANES - 1:SUBLANES, :]
        z1 = jnp.where(row == 0, c_m1, zr1)
        z2 = jnp.where(row == 0, c_m2, jnp.where(row == 1, c_m1, zr2))
        carry_ref[...] = z[tm - SUBLANES:tm, :]
    else:
        t = lax.rem(row, seq_len)
        z1 = jnp.where(t >= 1, zr1, p1_ref[...])
        z2 = jnp.where(t >= 2, zr2, p2_ref[...])
    zc = wconv_ref[0:1, :] * z2 + wconv_ref[1:2, :] * z1 + wconv_ref[2:3, :] * z
    cbz_ref[...] = (cb * zc).astype(BF16)

    g0 = c0 + 4 * conv_dim
    for c in range(N_BRANCH):
        g_ref[:, c * d:(c + 1) * d] = jax.nn.sigmoid(proj(g0 + c * d, g0 + (c + 1) * d))


def _in_proj(x, gmix, w_in, qn, kn, seg, wconv, conv_state, *, seq_len):
    n, d = x.shape
    conv_dim = wconv.shape[1]
    tm = min(ROW_TILE, n)
    carry_mode = seq_len >= tm
    assert n % tm == 0 and seq_len >= CONV_W - 1
    assert (seq_len % tm == 0) if carry_mode else (tm % seq_len == 0)
    n_seq = n // seq_len
    row = lambda w: pl.BlockSpec((tm, w), lambda i: (i, 0))
    full = lambda a: pl.BlockSpec(a.shape, lambda i: (0,) * a.ndim)
    if carry_mode:
        buf = jnp.zeros((n_seq, SUBLANES, conv_dim), F32).at[:, SUBLANES - 2:, :].set(conv_state)
        tiles_per_seq = seq_len // tm
        state_args = (buf,)
        state_specs = [pl.BlockSpec((None, SUBLANES, conv_dim), lambda i: (i // tiles_per_seq, 0, 0))]
        scratch = [pltpu.VMEM((SUBLANES, conv_dim), F32)]
    else:
        zeros = jnp.zeros((n_seq, seq_len, conv_dim), F32)
        p1 = zeros.at[:, 0].set(conv_state[:, 1]).reshape(n, conv_dim)
        p2 = zeros.at[:, 0].set(conv_state[:, 0]).at[:, 1].set(conv_state[:, 1]).reshape(n, conv_dim)
        state_args = (p1, p2)
        state_specs = [row(conv_dim), row(conv_dim)]
        scratch = []
    out_shape = (
        jax.ShapeDtypeStruct((n, d), BF16),
        jax.ShapeDtypeStruct((n, d), F32),
        jax.ShapeDtypeStruct((n, d), BF16),
        jax.ShapeDtypeStruct((n, d), F32),
        jax.ShapeDtypeStruct((n, d), BF16),
        jax.ShapeDtypeStruct((n, conv_dim), F32),
        jax.ShapeDtypeStruct((n, conv_dim), F32),
        jax.ShapeDtypeStruct((n, conv_dim), BF16),
        jax.ShapeDtypeStruct((n, N_BRANCH * d), F32),
    )
    out_specs = (row(d), row(d), row(d), row(d), row(d), row(conv_dim), row(conv_dim),
                 row(conv_dim), row(N_BRANCH * d))
    return pl.pallas_call(
        functools.partial(_in_proj_kernel, tm=tm, seq_len=seq_len, d_model=d, conv_dim=conv_dim,
                          carry_mode=carry_mode),
        grid=(n // tm,),
        in_specs=[row(d), full(gmix), _resident(), full(qn), full(kn), full(seg), full(wconv)] + state_specs,
        out_specs=out_specs,
        out_shape=out_shape,
        scratch_shapes=scratch,
        compiler_params=_cparams("arbitrary"),
        name="in_proj",
    )(x, gmix, w_in, qn, kn, seg, wconv, *state_args)


def _rel_bias(n, tbl_ref, head):
    max_exact = REL_BUCKETS // 2
    nc = jnp.maximum(n, 0)
    nf = jnp.maximum(nc, max_exact).astype(F32)
    large = max_exact + (jnp.log(nf / max_exact) / math.log(REL_MAX_DIST / max_exact)
                         * (REL_BUCKETS - max_exact)).astype(jnp.int32)
    large = jnp.minimum(large, REL_BUCKETS - 1)
    bucket = jnp.where(nc < max_exact, nc, large)
    val = jnp.zeros(n.shape, F32)
    for b in range(REL_BUCKETS):
        val = jnp.where(bucket == b, tbl_ref[b, head], val)
    return jnp.where(n >= 0, val, NEG)


def _prompt_bias_kernel(tbl_ref, o_ref, *, tile):
    head, d = pl.program_id(0), pl.program_id(1)
    key = lax.broadcasted_iota(jnp.int32, (tile, tile), 0)
    qry = lax.broadcasted_iota(jnp.int32, (tile, tile), 1)
    b = _rel_bias(d * tile + qry - key, tbl_ref, head)
    o_ref[...] = jnp.concatenate([b, b], axis=1)


def _prompt_bias(rel_table, tile):
    assert tile >= REL_MAX_DIST
    return pl.pallas_call(
        functools.partial(_prompt_bias_kernel, tile=tile),
        grid=(N_HEADS, 3),
        in_specs=[pl.BlockSpec(memory_space=pltpu.SMEM)],
        out_specs=pl.BlockSpec((None, None, tile, 2 * tile), lambda h, d: (h, d, 0, 0)),
        out_shape=jax.ShapeDtypeStruct((N_HEADS, 3, tile, 2 * tile), F32),
        compiler_params=_cparams("arbitrary", "arbitrary"),
        name="prompt_bias",
    )(rel_table)


def _sample_bias_kernel(tbl_ref, o_ref, onew_ref, *, past, t_new, blk):
    shape = (N_HEADS * 2 * t_new, blk)
    r = lax.broadcasted_iota(jnp.int32, shape, 0)
    col = lax.broadcasted_iota(jnp.int32, shape, 1) + pl.program_id(0) * blk
    t = lax.rem(r, t_new)
    n = past + t - col
    rn = lax.broadcasted_iota(jnp.int32, onew_ref.shape, 0)
    jn = lax.broadcasted_iota(jnp.int32, onew_ref.shape, 1)
    nn = jnp.where(jn < t_new, lax.rem(rn, t_new) - jn, -1)
    acc = jnp.zeros(shape, F32)
    accn = jnp.zeros(onew_ref.shape, F32)
    rows_per_head = 2 * t_new
    for head in range(N_HEADS):
        acc = jnp.where(r // rows_per_head == head, _rel_bias(n, tbl_ref, head), acc)
        accn = jnp.where(rn // rows_per_head == head, _rel_bias(nn, tbl_ref, head), accn)
    o_ref[...] = acc
    onew_ref[...] = accn


def _sample_bias(rel_table, past, t_new, blk):
    rows = N_HEADS * 2 * t_new
    return pl.pallas_call(
        functools.partial(_sample_bias_kernel, past=past, t_new=t_new, blk=blk),
        grid=(past // blk,),
        in_specs=[pl.BlockSpec(memory_space=pltpu.SMEM)],
        out_specs=(pl.BlockSpec((rows, blk), lambda s: (0, s)),
                   pl.BlockSpec((rows, SUBLANES), lambda s: (0, 0))),
        out_shape=(jax.ShapeDtypeStruct((rows, past), F32),
                   jax.ShapeDtypeStruct((rows, SUBLANES), F32)),
        compiler_params=_cparams("arbitrary"),
        name="sample_bias",
    )(rel_table)


def _lambda(lq1, lk1, lq2, lk2, lambda_init):
    s1 = jnp.sum(lq1[...] * lk1[...], axis=-1, keepdims=True)
    s2 = jnp.sum(lq2[...] * lk2[...], axis=-1, keepdims=True)
    return jnp.exp(s1) - jnp.exp(s2) + lambda_init


def _flash_kernel(q_ref, k_ref, vt_ref, bias_ref, sn_ref, lq1, lk1, lq2, lk2, o_ref, *, tile, lambda_init):
    i = pl.program_id(2)
    q = q_ref[...]
    lane = lax.broadcasted_iota(jnp.int32, q.shape, 1)
    zero = jnp.zeros_like(q)
    q2 = jnp.concatenate([jnp.where(lane < HEAD_DIM, q, zero), jnp.where(lane >= HEAD_DIM, q, zero)], axis=0)

    def body(j, carry):
        m, l, acc = carry
        start = pl.multiple_of(j * tile, tile)
        s = lax.dot_general(k_ref[pl.ds(start, tile), :], q2, (((1,), (1,)), ((), ())),
                            preferred_element_type=F32)
        s = s + bias_ref[jnp.minimum(i - j, 2)]
        m_new = jnp.maximum(m, jnp.max(s, axis=0, keepdims=True))
        alpha = jnp.exp(m - m_new)
        p = jnp.exp(s - m_new)
        l = alpha * l + jnp.sum(p, axis=0, keepdims=True)
        acc = alpha * acc + _dot(vt_ref[:, pl.ds(start, tile)], p.astype(BF16))
        return m_new, l, acc

    init = (jnp.full((1, 2 * tile), NEG, F32), jnp.zeros((1, 2 * tile), F32),
            jnp.zeros((V_DIM, 2 * tile), F32))
    _, l, acc = lax.fori_loop(0, i + 1, body, init)
    o = acc * (1.0 / l)
    lam = _lambda(lq1, lk1, lq2, lk2, lambda_init)
    od = o[:, :tile] - lam * o[:, tile:]
    ms = jnp.mean(od * od, axis=0, keepdims=True)
    on = od * lax.rsqrt(ms + EPS) * sn_ref[...]
    o_ref[...] = on.T.astype(BF16)


def _prompt_attention(q, k, vt, bias, sn, lq1, lk1, lq2, lk2, *, batch, seq_len, lambda_init):
    tile = ATTN_TILE
    nq = seq_len // tile
    assert seq_len % tile == 0
    small = pl.BlockSpec((1, HEAD_DIM), lambda b, h, i: (0, 0))
    return pl.pallas_call(
        functools.partial(_flash_kernel, tile=tile, lambda_init=lambda_init),
        grid=(batch, N_HEADS, nq),
        in_specs=[
            pl.BlockSpec((tile, V_DIM), lambda b, h, i: (b * nq + i, h)),
            pl.BlockSpec((seq_len, V_DIM), lambda b, h, i: (b, h)),
            pl.BlockSpec((None, None, V_DIM, seq_len), lambda b, h, i: (b, h, 0, 0)),
            pl.BlockSpec((None, 3, tile, 2 * tile), lambda b, h, i: (h, 0, 0, 0)),
            pl.BlockSpec((V_DIM, tile), lambda b, h, i: (0, 0)),
            small, small, small, small,
        ],
        out_specs=pl.BlockSpec((tile, V_DIM), lambda b, h, i: (b * nq + i, h)),
        out_shape=jax.ShapeDtypeStruct(q.shape, BF16),
        compiler_params=_cparams("arbitrary", "arbitrary", "arbitrary"),
        name="prompt_attention",
    )(q, k, vt, bias, sn, lq1, lk1, lq2, lk2)


def _paged_kernel(*refs, n_pages, t_new, d_model, lambda_init):
    pt_ref = refs[0]
    del pt_ref
    k_refs = refs[1:1 + n_pages]
    v_refs = refs[1 + n_pages:1 + 2 * n_pages]
    (q_ref, bias_ref, kn_ref, vn_ref, bnew_ref, sn_ref, lq1, lk1, lq2, lk2,
     o_ref, m_ref, l_ref, acc_ref) = refs[1 + 2 * n_pages:]
    step = pl.program_id(1)
    rows = N_HEADS * 2 * t_new

    r = lax.broadcasted_iota(jnp.int32, (rows, d_model), 0)
    lane = lax.broadcasted_iota(jnp.int32, (rows, d_model), 1)
    own = (lane // HEAD_DIM) == (r // t_new)
    qbd = jnp.where(own, q_ref[...], jnp.zeros((rows, d_model), BF16))

    @pl.when(step == 0)
    def _():
        m_ref[...] = jnp.full(m_ref.shape, NEG, F32)
        l_ref[...] = jnp.zeros(l_ref.shape, F32)
        acc_ref[...] = jnp.zeros(acc_ref.shape, F32)

    def scores(kb):
        return lax.dot_general(qbd, kb, (((1,), (1,)), ((), ())), preferred_element_type=F32)

    def update(s, pv_fn):
        m = m_ref[...]
        m_new = jnp.maximum(m, jnp.max(s, axis=-1, keepdims=True))
        alpha = jnp.exp(m - m_new)
        p = jnp.exp(s - m_new)
        l_ref[...] = alpha * l_ref[...] + jnp.sum(p, axis=-1, keepdims=True)
        acc_ref[...] = alpha * acc_ref[...] + pv_fn(p.astype(BF16))
        m_ref[...] = m_new

    page = k_refs[0].shape[0]
    s = jnp.concatenate([scores(kr[...].astype(BF16)) for kr in k_refs], axis=1) + bias_ref[...]

    def pv_pages(p):
        out = None
        for c, vr in enumerate(v_refs):
            o = _dot(p[:, c * page:(c + 1) * page], vr[...].astype(BF16))
            out = o if out is None else out + o
        return out

    update(s, pv_pages)

    @pl.when(step == pl.num_programs(1) - 1)
    def _():
        update(scores(kn_ref[...]) + bnew_ref[...], lambda p: _dot(p, vn_ref[...]))
        a = acc_ref[...] * (1.0 / l_ref[...])
        a2 = pltpu.roll(a, rows - t_new, axis=0)
        lam = _lambda(lq1, lk1, lq2, lk2, lambda_init)
        head_lanes = (lane // V_DIM) == (r // (2 * t_new))
        dm = jnp.where(head_lanes, a - lam * a2, 0.0)
        ms = jnp.sum(dm * dm, axis=-1, keepdims=True) * (1.0 / V_DIM)
        dn = dm * lax.rsqrt(ms + EPS) * sn_ref[...]
        o_ref[...] = jnp.sum(dn.reshape(N_HEADS, 2 * t_new, d_model), axis=0).astype(BF16)


def _sample_attention(q_rep, cache_k, cache_v, page_table, bias, k_new, v_new, bias_new, sn,
                      lq1, lk1, lq2, lk2, *, layer, t_new, lambda_init):
    db, n_tbl = page_table.shape
    _, _, page, d = cache_k.shape
    rows = N_HEADS * 2 * t_new
    npg = PAGES_PER_STEP
    assert n_tbl % npg == 0 and 2 * t_new == SUBLANES

    def page_spec(c):
        return pl.BlockSpec((None, None, page, d), lambda b, s, pt: (layer, pt[b, s * npg + c], 0, 0))

    small = pl.BlockSpec((1, HEAD_DIM), lambda b, s, pt: (0, 0))
    in_specs = [page_spec(c) for c in range(npg)] * 2 + [
        pl.BlockSpec((None, rows, d), lambda b, s, pt: (b, 0, 0)),
        pl.BlockSpec((rows, npg * page), lambda b, s, pt: (0, s)),
        pl.BlockSpec((None, SUBLANES, d), lambda b, s, pt: (b, 0, 0)),
        pl.BlockSpec((None, SUBLANES, d), lambda b, s, pt: (b, 0, 0)),
        pl.BlockSpec((rows, SUBLANES), lambda b, s, pt: (0, 0)),
        pl.BlockSpec((1, d), lambda b, s, pt: (0, 0)),
        small, small, small, small,
    ]
    grid_spec = pltpu.PrefetchScalarGridSpec(
        num_scalar_prefetch=1,
        grid=(db, n_tbl // npg),
        in_specs=in_specs,
        out_specs=pl.BlockSpec((None, SUBLANES, d), lambda b, s, pt: (b, 0, 0)),
        scratch_shapes=[pltpu.VMEM((rows, 1), F32), pltpu.VMEM((rows, 1), F32), pltpu.VMEM((rows, d), F32)],
    )
    return pl.pallas_call(
        functools.partial(_paged_kernel, n_pages=npg, t_new=t_new, d_model=d, lambda_init=lambda_init),
        grid_spec=grid_spec,
        out_shape=jax.ShapeDtypeStruct((db, SUBLANES, d), BF16),
        compiler_params=_cparams("arbitrary", "arbitrary"),
        name="sample_attention",
    )(page_table, *([cache_k] * npg), *([cache_v] * npg), q_rep, bias, k_new, v_new, bias_new, sn,
      lq1, lk1, lq2, lk2)


def _ssm_param_kernel(are_ref, aim_ref, ldt_ref, bre_ref, bim_ref, lre_ref, lim_ref, bbre_ref, bbim_ref):
    ar, ai = are_ref[...], aim_ref[...]
    dt = jnp.exp(ldt_ref[...])
    mag = jnp.exp(ar * dt)
    lr = mag * jnp.cos(ai * dt)
    li = mag * jnp.sin(ai * dt)
    lre_ref[...] = lr
    lim_ref[...] = li
    nr, ni = lr - 1.0, li
    den = ar * ar + ai * ai
    fr = ((nr * ar + ni * ai) / den)[:, None, :]
    fi = ((ni * ar - nr * ai) / den)[:, None, :]
    br, bi = bre_ref[...], bim_ref[...]
    bbre_ref[...] = fr * br - fi * bi
    bbim_ref[...] = fr * bi + fi * br


def _ssm_params(a_re, a_im, log_dt, b_re_t, b_im_t):
    g, p = a_re.shape
    c = b_re_t.shape[1]
    return pl.pallas_call(
        _ssm_param_kernel,
        out_shape=(jax.ShapeDtypeStruct((g, p), F32), jax.ShapeDtypeStruct((g, p), F32),
                   jax.ShapeDtypeStruct((g, c, p), F32), jax.ShapeDtypeStruct((g, c, p), F32)),
        name="ssm_params",
    )(a_re, a_im, log_dt.reshape(g, 1), b_re_t, b_im_t)


def _ssm_kernel(u_ref, h0_ref, lre_ref, lim_ref, bre_ref, bim_ref, cre_ref, cim_ref, d_ref,
                y_ref, hfin_ref, hbuf, carry, *, batch, state_w):
    rc = u_ref.shape[0]
    sw = state_w

    @pl.when(pl.program_id(0) == 0)
    def _():
        carry[...] = h0_ref[...]

    u = u_ref[...]
    ub = u.astype(BF16)
    n_in = bre_ref.shape[0]
    kin, nout = bre_ref.shape[1], bre_ref.shape[2]
    for j in range(n_in):
        uj = ub[:, j * kin:(j + 1) * kin]
        hbuf[:, j * nout:(j + 1) * nout] = _dot(uj, bre_ref[j])
        hbuf[:, sw + j * nout:sw + (j + 1) * nout] = _dot(uj, bim_ref[j])

    rows = carry.shape[0]
    lr = jnp.broadcast_to(lre_ref[...], (rows, sw))
    li = jnp.broadcast_to(lim_ref[...], (rows, sw))

    def step(cr, ci, xr, xi):
        return lr * cr - li * ci + xr, lr * ci + li * cr + xi

    if batch == rows:
        def body(t, c):
            cr, ci = c
            r0 = pl.multiple_of(t * rows, rows)
            hr, hi = step(cr, ci, hbuf[pl.ds(r0, rows), :sw], hbuf[pl.ds(r0, rows), sw:])
            hbuf[pl.ds(r0, rows), :sw] = hr
            hbuf[pl.ds(r0, rows), sw:] = hi
            return hr, hi
    else:
        assert 2 * batch == rows
        upper = lax.broadcasted_iota(jnp.int32, (rows, sw), 0) >= batch

        def body(t, c):
            cr, ci = c
            r0 = pl.multiple_of(t * rows, rows)
            xr, xi = hbuf[pl.ds(r0, rows), :sw], hbuf[pl.ds(r0, rows), sw:]
            ar, ai = step(pltpu.roll(cr, batch, axis=0), pltpu.roll(ci, batch, axis=0), xr, xi)
            br, bi = step(pltpu.roll(ar, batch, axis=0), pltpu.roll(ai, batch, axis=0), xr, xi)
            hr, hi = jnp.where(upper, br, ar), jnp.where(upper, bi, ai)
            hbuf[pl.ds(r0, rows), :sw] = hr
            hbuf[pl.ds(r0, rows), sw:] = hi
            return hr, hi

    cr, ci = lax.fori_loop(0, rc // rows, body, (carry[:, :sw], carry[:, sw:]))
    carry[:, :sw] = cr
    carry[:, sw:] = ci
    hfin_ref[:, :sw] = cr
    hfin_ref[:, sw:] = ci

    n_out = cre_ref.shape[0]
    kout, wout = cre_ref.shape[1], cre_ref.shape[2]
    ys = []
    for o in range(n_out):
        hr = hbuf[:, o * kout:(o + 1) * kout].astype(BF16)
        hi = hbuf[:, sw + o * kout:sw + (o + 1) * kout].astype(BF16)
        ys.append(_dot(hr, cre_ref[o]) + _dot(hi, cim_ref[o]))
    y = jnp.concatenate(ys, axis=1) + d_ref[...] * u
    y_ref[...] = jax.nn.gelu(y).astype(BF16)
    del wout


def _ssm(u_tb, h0, lre, lim, bre, bim, cre, cim, dskip, *, batch):
    n, ch = u_tb.shape
    sw = lre.shape[1]
    rows = h0.shape[0]
    rc = min(SSM_CHUNK_ROWS, n)
    assert n % rc == 0 and rc % rows == 0
    full = lambda a: pl.BlockSpec(a.shape, lambda i: (0,) * a.ndim)
    return pl.pallas_call(
        functools.partial(_ssm_kernel, batch=batch, state_w=sw),
        grid=(n // rc,),
        in_specs=[pl.BlockSpec((rc, ch), lambda i: (i, 0)), full(h0), full(lre), full(lim),
                  full(bre), full(bim), full(cre), full(cim), full(dskip)],
        out_specs=(pl.BlockSpec((rc, ch), lambda i: (i, 0)), full(h0)),
        out_shape=(jax.ShapeDtypeStruct((n, ch), BF16), jax.ShapeDtypeStruct(h0.shape, F32)),
        scratch_shapes=[pltpu.VMEM((rc, 2 * sw), F32), pltpu.VMEM(h0.shape, F32)],
        compiler_params=_cparams("arbitrary"),
        name="ssm",
    )(u_tb, h0, lre, lim, bre, bim, cre, cim, dskip)


def _merge_kernel(x_ref, o_ref, ys_ref, cbz_ref, g_ref, wao_ref, wga_ref, wgb_ref, wco_ref, wo_ref, out_ref):
    d = x_ref.shape[1]
    ys = ys_ref[...]
    attn_out = _dot(o_ref[...], wao_ref[...])
    ssm_out = _dot(ys, wga_ref[...]) * jax.nn.sigmoid(_dot(ys, wgb_ref[...]))
    conv_out = _dot(cbz_ref[...], wco_ref[...])
    merged = (g_ref[:, 0:d] * attn_out + g_ref[:, d:2 * d] * ssm_out + g_ref[:, 2 * d:3 * d] * conv_out)
    out_ref[...] = x_ref[...] + _dot(merged.astype(BF16), wo_ref[...])


def _merge(x, o, ys, cbz, g, wao, wga, wgb, wco, wo):
    n, d = x.shape
    tm = min(ROW_TILE, n)
    row = lambda a: pl.BlockSpec((tm, a.shape[1]), lambda i: (i, 0))
    return pl.pallas_call(
        _merge_kernel,
        grid=(n // tm,),
        in_specs=[row(x), row(o), row(ys), row(cbz), row(g)] + [_resident()] * 5,
        out_specs=row(x),
        out_shape=jax.ShapeDtypeStruct((n, d), F32),
        compiler_params=_cparams("parallel"),
        name="merge",
    )(x, o, ys, cbz, g, wao, wga, wgb, wco, wo)


def _ffn_kernel(x_ref, pe_ref, gffn_ref, wg_ref, wu_ref, wd_ref, gple_ref, wpg_ref, wpp_ref, out_ref):
    x = x_ref[...]
    h = _rms(x, gffn_ref[...]).astype(BF16)
    act = jax.nn.silu(_dot(h, wg_ref[...])) * _dot(h, wu_ref[...])
    x = x + _dot(act.astype(BF16), wd_ref[...])
    h = _rms(x, gple_ref[...]).astype(BF16)
    gate = jax.nn.sigmoid(_dot(h, wpg_ref[...]))
    out_ref[...] = x + gate * _dot(pe_ref[...].astype(BF16), wpp_ref[...])


def _ffn(x, pe, gffn, wg, wu, wd, gple, wpg, wpp):
    n, d = x.shape
    tm = min(ROW_TILE, n)
    row = lambda a: pl.BlockSpec((tm, a.shape[1]), lambda i: (i, 0))
    full = lambda a: pl.BlockSpec(a.shape, lambda i: (0,) * a.ndim)
    return pl.pallas_call(
        _ffn_kernel,
        grid=(n // tm,),
        in_specs=[row(x), row(pe), full(gffn), _resident(), _resident(), _resident(), full(gple),
                  _resident(), _resident()],
        out_specs=row(x),
        out_shape=jax.ShapeDtypeStruct((n, d), F32),
        compiler_params=_cparams("parallel"),
        name="ffn",
    )(x, pe, gffn, wg, wu, wd, gple, wpg, wpp)


def _block_diag(blocks):
    n, r, c = blocks.shape
    eye = jnp.eye(n, dtype=bool)
    return jnp.where(eye[:, None, :, None], blocks[:, :, None, :], 0).reshape(n * r, n * c)


def _ssm_matrices(lw):
    g, p = lw["a_re"].shape
    lre, lim, bbre, bbim = _ssm_params(lw["a_re"], lw["a_im"], lw["log_dt"],
                                       jnp.swapaxes(lw["b_re"], 1, 2), jnp.swapaxes(lw["b_im"], 1, 2))
    c = bbre.shape[1]
    gin = 128 // c
    b_in = lambda bb: jax.vmap(_block_diag)(bb.reshape(g // gin, gin, c, p)).astype(BF16)
    gout = MXU_DIM // c
    c_out = lambda cc: jax.vmap(_block_diag)(
        jnp.swapaxes(cc, 1, 2).reshape(g // gout, gout, p, c)).astype(BF16)
    return (lre.reshape(1, g * p), lim.reshape(1, g * p), b_in(bbre), b_in(bbim),
            c_out(lw["c_re"]), c_out(-lw["c_im"]))


def _layer(x, pe, attend, h0_re, h0_im, conv_state, lw, mats, lambda_init, *, batch, seq_len):
    n, d = x.shape
    outs = _in_proj(x, lw["g_mix"], lw["w_in"], lw["qn"], lw["kn"], lw["seg"], lw["w_conv"], conv_state,
                    seq_len=seq_len)
    q, k_f, k_b, v_f, v_b, u, z, cbz, gates = outs
    o = attend(q, k_b, v_b)

    lre, lim, bre, bim, cre, cim = mats
    sw = lre.shape[1]
    ch = u.shape[1]
    u_tb = u.reshape(batch, seq_len, ch).swapaxes(0, 1).reshape(n, ch)
    h0 = jnp.concatenate([h0_re.reshape(batch, sw), h0_im.reshape(batch, sw)], axis=1)
    rows = max(batch, SUBLANES)
    if batch < rows:
        h0 = jnp.concatenate([jnp.zeros((rows - batch, 2 * sw), F32), h0], axis=0)
    ys_tb, hfin = _ssm(u_tb, h0, lre, lim, bre, bim, cre, cim, lw["d_skip"], batch=batch)
    ys = ys_tb.reshape(seq_len, batch, ch).swapaxes(0, 1).reshape(n, ch)
    hfin = hfin[rows - batch:]
    groups = sw // SSM_STATE

    x1 = _merge(x, o, ys, cbz, gates, lw["w_attn_out"], lw["w_glu_a"], lw["w_glu_b"], lw["w_conv_out"], lw["w_o"])
    x2 = _ffn(x1, pe, lw["g_ffn"], lw["w_ffn_gate"], lw["w_ffn_up"], lw["w_ffn_down"], lw["g_ple"],
              lw["w_ple_gate"], lw["w_ple_proj"])
    return (x2,
            k_f.reshape(batch, seq_len, N_HEADS, V_DIM),
            v_f.reshape(batch, seq_len, N_HEADS, V_DIM),
            hfin[:, :sw].reshape(batch, groups, SSM_STATE),
            hfin[:, sw:].reshape(batch, groups, SSM_STATE),
            z.reshape(batch, seq_len, ch)[:, seq_len - (CONV_W - 1):],
            k_b, v_b)


def kernel(x_prompt, x_sample, cache_k, cache_v, state_ssm_re, state_ssm_im, state_conv, page_table, p_prompt, p_sample, rel_table, g_mix, w_in, q_norm, k_norm, lambda_q1, lambda_k1, lambda_q2, lambda_k2, sub_norm, w_attn_out, ssm_a_re, ssm_a_im, ssm_log_dt, ssm_b_re, ssm_b_im, ssm_c_re, ssm_c_im, ssm_d, w_glu_a, w_glu_b, w_conv, w_conv_out, w_o, g_ffn, w_ffn_gate, w_ffn_up, w_ffn_down, g_ple, w_ple_gate, w_ple_proj):
    bp, seq, d = x_prompt.shape
    db, t_new, _ = x_sample.shape
    depth = w_in.shape[0]
    n_pool, page = cache_k.shape[1], cache_k.shape[2]
    past = page_table.shape[1] * page
    conv_dim = w_conv.shape[2]

    cache_k2 = cache_k.reshape(depth, n_pool, page, d)
    cache_v2 = cache_v.reshape(depth, n_pool, page, d)
    bias_p = _prompt_bias(rel_table, ATTN_TILE)
    bias_s, bias_new = _sample_bias(rel_table, past, t_new, PAGES_PER_STEP * page)
    seg = (_block_diag(jnp.ones((MXU_DIM // HEAD_DIM, HEAD_DIM, HEAD_DIM), F32)) / HEAD_DIM).astype(BF16)
    reps = d // HEAD_DIM

    xp = x_prompt.reshape(bp * seq, d)
    xs = x_sample.reshape(db * t_new, d)
    outs_p, outs_s = [], []
    for i in range(depth):
        lambda_init = 0.8 - 0.6 * math.exp(-0.3 * i)
        bf = lambda w: w[i].astype(BF16)
        vec = lambda w: w[i].reshape(1, -1)
        lw = dict(
            g_mix=vec(g_mix), w_in=bf(w_in), w_conv=w_conv[i], seg=seg,
            qn=jnp.tile(q_norm[i], reps).reshape(1, d) * (HEAD_DIM ** -0.5),
            kn=jnp.tile(k_norm[i], reps).reshape(1, d),
            w_attn_out=bf(w_attn_out), w_glu_a=bf(w_glu_a), w_glu_b=bf(w_glu_b),
            w_conv_out=bf(w_conv_out), w_o=bf(w_o), g_ffn=vec(g_ffn), w_ffn_gate=bf(w_ffn_gate),
            w_ffn_up=bf(w_ffn_up), w_ffn_down=bf(w_ffn_down), g_ple=vec(g_ple),
            w_ple_gate=bf(w_ple_gate), w_ple_proj=bf(w_ple_proj),
            a_re=ssm_a_re[i], a_im=ssm_a_im[i], log_dt=ssm_log_dt[i], b_re=ssm_b_re[i], b_im=ssm_b_im[i],
            c_re=ssm_c_re[i], c_im=ssm_c_im[i], d_skip=vec(ssm_d),
        )
        mats = _ssm_matrices(lw)
        lam_args = tuple(w[i].reshape(1, HEAD_DIM) for w in (lambda_q1, lambda_k1, lambda_q2, lambda_k2))
        sn_scaled = sub_norm[i] * (1.0 - lambda_init)

        def attend_prompt(q, k_b, v_b):
            vt = v_b.reshape(bp, seq, N_HEADS, V_DIM).transpose(0, 2, 3, 1)
            sn = jnp.broadcast_to(sn_scaled[:, None], (V_DIM, ATTN_TILE))
            return _prompt_attention(q, k_b, vt, bias_p, sn, *lam_args, batch=bp, seq_len=seq,
                                     lambda_init=lambda_init)

        def attend_sample(q, k_b, v_b):
            rows = N_HEADS * 2 * t_new
            q_rep = jnp.tile(q.reshape(db, t_new, d), (1, rows // t_new, 1))
            pad = lambda a: jnp.pad(a.reshape(db, t_new, d), ((0, 0), (0, SUBLANES - t_new), (0, 0)))
            sn = jnp.tile(sn_scaled, N_HEADS).reshape(1, d)
            o = _sample_attention(q_rep, cache_k2, cache_v2, page_table, bias_s, pad(k_b), pad(v_b),
                                  bias_new, sn, *lam_args, layer=i, t_new=t_new, lambda_init=lambda_init)
            return o[:, :t_new].reshape(db * t_new, d)

        zs = jnp.zeros((bp, d // 2 // SSM_GROUP * SSM_STATE), F32)
        rp = _layer(xp, p_prompt[i].reshape(bp * seq, -1), attend_prompt, zs, zs,
                    jnp.zeros((bp, CONV_W - 1, conv_dim), F32), lw, mats, lambda_init, batch=bp, seq_len=seq)
        rs = _layer(xs, p_sample[i].reshape(db * t_new, -1), attend_sample, state_ssm_re[i], state_ssm_im[i],
                    state_conv[i], lw, mats, lambda_init, batch=db, seq_len=t_new)
        xp, xs = rp[0], rs[0]
        outs_p.append(rp[1:6])
        outs_s.append(rs[1:6])

    stack = lambda outs, j: jnp.stack([o[j] for o in outs])
    return ((xp.reshape(bp, seq, d), xs.reshape(db, t_new, d))
            + tuple(stack(outs_p, j) for j in range(5))
            + tuple(stack(outs_s, j) for j in range(5)))
```

```python
import functools
import math

import jax
import jax.numpy as jnp
from jax import lax
from jax.experimental import pallas as pl
from jax.experimental.pallas import tpu as pltpu

F32 = jnp.float32
BF16 = jnp.bfloat16

N_HEADS = 8
HEAD_DIM = 64
V_DIM = 2 * HEAD_DIM
SSM_GROUP = 16
SSM_STATE = 64
CONV_W = 3
N_BRANCH = 3
REL_BUCKETS = 32
REL_MAX_DIST = 128
EPS = 1e-6
NEG = -1e30
LOG2E = 1.4426950408889634

V7X_VMEM_BYTES = 64 * 1024 * 1024
VMEM_LIMIT_BYTES = (V7X_VMEM_BYTES * 7) // 8
MXU_DIM = 256
SUBLANES = 8
ROW_TILE = 256
ATTN_TILE = 256
ATTN_HEADS_PER_STEP = 4
SSM_CHUNK_ROWS = 256
PAGES_PER_STEP = 8


def _cparams(*sem):
    return pltpu.CompilerParams(dimension_semantics=sem, vmem_limit_bytes=VMEM_LIMIT_BYTES)


def _resident():
    return pl.BlockSpec(memory_space=pltpu.VMEM)


def _rms(x, g):
    ms = jnp.mean(x * x, axis=-1, keepdims=True)
    return x * lax.rsqrt(ms + EPS) * g


def _dot(a, b):
    return jnp.dot(a, b, preferred_element_type=F32)


def _head_norm(acc, seg, gain):
    outs = []
    for c in range(acc.shape[1] // MXU_DIM):
        a = acc[:, c * MXU_DIM:(c + 1) * MXU_DIM]
        ms = _dot((a * a).astype(BF16), seg)
        outs.append(a * lax.rsqrt(ms + EPS))
    return jnp.concatenate(outs, axis=1) * gain


def _in_proj_kernel(*refs, tm, seq_len, d_model, conv_dim, carry_mode):
    if carry_mode:
        (x_ref, gmix_ref, w_ref, qn_ref, kn_ref, seg_ref, wconv_ref, buf_ref,
         q_ref, kf_ref, kb_ref, vf_ref, vb_ref, u_ref, z_ref, cbz_ref, g_ref, carry_ref) = refs
    else:
        (x_ref, gmix_ref, w_ref, qn_ref, kn_ref, seg_ref, wconv_ref, p1_ref, p2_ref,
         q_ref, kf_ref, kb_ref, vf_ref, vb_ref, u_ref, z_ref, cbz_ref, g_ref) = refs
    d = d_model
    h = _rms(x_ref[...], gmix_ref[...]).astype(BF16)

    def proj(c0, c1):
        return _dot(h, w_ref[:, c0:c1])

    seg = seg_ref[...]
    q_ref[...] = _head_norm(proj(0, d), seg, qn_ref[...]).astype(BF16)
    k = _head_norm(proj(d, 2 * d), seg, kn_ref[...])
    kf_ref[...] = k
    kb_ref[...] = k.astype(BF16)
    v = proj(2 * d, 3 * d)
    vf_ref[...] = v
    vb_ref[...] = v.astype(BF16)
    c0 = 3 * d
    ucb = proj(c0, c0 + 2 * conv_dim)
    u_ref[...] = ucb[:, :conv_dim]
    cb = ucb[:, conv_dim:]
    ccx = proj(c0 + 2 * conv_dim, c0 + 4 * conv_dim)
    z = ccx[:, :conv_dim] * ccx[:, conv_dim:]
    z_ref[...] = z

    row = lax.broadcasted_iota(jnp.int32, z.shape, 0)
    zr1 = pltpu.roll(z, 1, axis=0)
    zr2 = pltpu.roll(z, 2, axis=0)
    if carry_mode:
        tiles_per_seq = seq_len // tm

        @pl.when(pl.program_id(0) % tiles_per_seq == 0)
        def _():
            carry_ref[...] = buf_ref[...]

        c_m2 = carry_ref[SUBLANES - 2:SUBLANES - 1, :]
        c_m1 = carry_ref[SUBLANES - 1:SUBLANES, :]
        z1 = jnp.where(row == 0, c_m1, zr1)
        z2 = jnp.where(row == 0, c_m2, jnp.where(row == 1, c_m1, zr2))
        carry_ref[...] = z[tm - SUBLANES:tm, :]
    else:
        t = lax.rem(row, seq_len)
        z1 = jnp.where(t >= 1, zr1, p1_ref[...])
        z2 = jnp.where(t >= 2, zr2, p2_ref[...])
    zc = wconv_ref[0:1, :] * z2 + wconv_ref[1:2, :] * z1 + wconv_ref[2:3, :] * z
    cbz_ref[...] = (cb * zc).astype(BF16)

    g0 = c0 + 4 * conv_dim
    for c in range(N_BRANCH):
        g_ref[:, c * d:(c + 1) * d] = jax.nn.sigmoid(proj(g0 + c * d, g0 + (c + 1) * d))


def _in_proj(x, gmix, w_in, qn, kn, seg, wconv, conv_state, *, seq_len):
    n, d = x.shape
    conv_dim = wconv.shape[1]
    tm = min(ROW_TILE, n)
    carry_mode = seq_len >= tm
    assert n % tm == 0 and seq_len >= CONV_W - 1
    assert (seq_len % tm == 0) if carry_mode else (tm % seq_len == 0)
    n_seq = n // seq_len
    row = lambda w: pl.BlockSpec((tm, w), lambda i: (i, 0))
    full = lambda a: pl.BlockSpec(a.shape, lambda i: (0,) * a.ndim)
    if carry_mode:
        buf = jnp.zeros((n_seq, SUBLANES, conv_dim), F32).at[:, SUBLANES - 2:, :].set(conv_state)
        tiles_per_seq = seq_len // tm
        state_args = (buf,)
        state_specs = [pl.BlockSpec((None, SUBLANES, conv_dim), lambda i: (i // tiles_per_seq, 0, 0))]
        scratch = [pltpu.VMEM((SUBLANES, conv_dim), F32)]
    else:
        zeros = jnp.zeros((n_seq, seq_len, conv_dim), F32)
        p1 = zeros.at[:, 0].set(conv_state[:, 1]).reshape(n, conv_dim)
        p2 = zeros.at[:, 0].set(conv_state[:, 0]).at[:, 1].set(conv_state[:, 1]).reshape(n, conv_dim)
        state_args = (p1, p2)
        state_specs = [row(conv_dim), row(conv_dim)]
        scratch = []
    out_shape = (
        jax.ShapeDtypeStruct((n, d), BF16),
        jax.ShapeDtypeStruct((n, d), F32),
        jax.ShapeDtypeStruct((n, d), BF16),
        jax.ShapeDtypeStruct((n, d), F32),
        jax.ShapeDtypeStruct((n, d), BF16),
        jax.ShapeDtypeStruct((n, conv_dim), F32),
        jax.ShapeDtypeStruct((n, conv_dim), F32),
        jax.ShapeDtypeStruct((n, conv_dim), BF16),
        jax.ShapeDtypeStruct((n, N_BRANCH * d), F32),
    )
    out_specs = (row(d), row(d), row(d), row(d), row(d), row(conv_dim), row(conv_dim),
                 row(conv_dim), row(N_BRANCH * d))
    return pl.pallas_call(
        functools.partial(_in_proj_kernel, tm=tm, seq_len=seq_len, d_model=d, conv_dim=conv_dim,
                          carry_mode=carry_mode),
        grid=(n // tm,),
        in_specs=[row(d), full(gmix), _resident(), full(qn), full(kn), full(seg), full(wconv)] + state_specs,
        out_specs=out_specs,
        out_shape=out_shape,
        scratch_shapes=scratch,
        compiler_params=_cparams("arbitrary"),
        name="in_proj",
    )(x, gmix, w_in, qn, kn, seg, wconv, *state_args)


def _rel_bias(n, tbl_ref, head):
    max_exact = REL_BUCKETS // 2
    nc = jnp.maximum(n, 0)
    nf = jnp.maximum(nc, max_exact).astype(F32)
    large = max_exact + (jnp.log(nf / max_exact) / math.log(REL_MAX_DIST / max_exact)
                         * (REL_BUCKETS - max_exact)).astype(jnp.int32)
    large = jnp.minimum(large, REL_BUCKETS - 1)
    bucket = jnp.where(nc < max_exact, nc, large)
    val = jnp.zeros(n.shape, F32)
    for b in range(REL_BUCKETS):
        val = jnp.where(bucket == b, tbl_ref[b, head], val)
    return jnp.where(n >= 0, val * LOG2E, NEG)


def _prompt_bias_kernel(tbl_ref, o_ref, *, tile):
    head, d = pl.program_id(0), pl.program_id(1)
    key = lax.broadcasted_iota(jnp.int32, (tile, tile), 0)
    qry = lax.broadcasted_iota(jnp.int32, (tile, tile), 1)
    b = _rel_bias(d * tile + qry - key, tbl_ref, head)
    o_ref[...] = jnp.concatenate([b, b], axis=1)


def _prompt_bias(rel_table, tile):
    assert tile >= REL_MAX_DIST
    return pl.pallas_call(
        functools.partial(_prompt_bias_kernel, tile=tile),
        grid=(N_HEADS, 3),
        in_specs=[pl.BlockSpec(memory_space=pltpu.SMEM)],
        out_specs=pl.BlockSpec((None, None, tile, 2 * tile), lambda h, d: (h, d, 0, 0)),
        out_shape=jax.ShapeDtypeStruct((N_HEADS, 3, tile, 2 * tile), F32),
        compiler_params=_cparams("arbitrary", "arbitrary"),
        name="prompt_bias",
    )(rel_table)


def _sample_bias_kernel(tbl_ref, o_ref, onew_ref, *, past, t_new, page):
    shape = (N_HEADS * 2 * t_new, page * N_HEADS)
    r = lax.broadcasted_iota(jnp.int32, shape, 0)
    col = lax.broadcasted_iota(jnp.int32, shape, 1)
    first_key = jnp.where(pl.program_id(0) == 0, 0, past - page)
    n = past + lax.rem(r, t_new) - (first_key + col // N_HEADS)
    rn = lax.broadcasted_iota(jnp.int32, onew_ref.shape, 0)
    jn = lax.broadcasted_iota(jnp.int32, onew_ref.shape, 1)
    nn = jnp.where(jn < t_new, lax.rem(rn, t_new) - jn, -1)
    acc = jnp.full(shape, NEG, F32)
    accn = jnp.zeros(onew_ref.shape, F32)
    rows_per_head = 2 * t_new
    for head in range(N_HEADS):
        own = (r // rows_per_head == head) & (lax.rem(col, N_HEADS) == head)
        acc = jnp.where(own, _rel_bias(n, tbl_ref, head), acc)
        accn = jnp.where(rn // rows_per_head == head, _rel_bias(nn, tbl_ref, head), accn)
    o_ref[...] = acc
    onew_ref[...] = accn


def _sample_bias(rel_table, past, t_new, page):
    assert page >= REL_MAX_DIST and past >= 2 * page
    rows = N_HEADS * 2 * t_new
    return pl.pallas_call(
        functools.partial(_sample_bias_kernel, past=past, t_new=t_new, page=page),
        grid=(2,),
        in_specs=[pl.BlockSpec(memory_space=pltpu.SMEM)],
        out_specs=(pl.BlockSpec((None, rows, page * N_HEADS), lambda s: (s, 0, 0)),
                   pl.BlockSpec((rows, SUBLANES), lambda s: (0, 0))),
        out_shape=(jax.ShapeDtypeStruct((2, rows, page * N_HEADS), F32),
                   jax.ShapeDtypeStruct((rows, SUBLANES), F32)),
        compiler_params=_cparams("arbitrary"),
        name="sample_bias",
    )(rel_table)


def _lambda(lq1, lk1, lq2, lk2, lambda_init):
    s1 = jnp.sum(lq1[...] * lk1[...], axis=-1, keepdims=True)
    s2 = jnp.sum(lq2[...] * lk2[...], axis=-1, keepdims=True)
    return jnp.exp(s1) - jnp.exp(s2) + lambda_init


def _flash_kernel(q_ref, k_ref, vt_ref, bias_ref, sn_ref, lq1, lk1, lq2, lk2, o_ref, m_ref, l_ref, acc_ref,
                  *, tile, lambda_init):
    i = pl.program_id(2)
    heads = m_ref.shape[0]
    lane = lax.broadcasted_iota(jnp.int32, (tile, V_DIM), 1)
    zero = jnp.zeros((tile, V_DIM), BF16)
    q2 = []
    for hh in range(heads):
        q = q_ref[:, hh * V_DIM:(hh + 1) * V_DIM]
        q2.append(jnp.concatenate([jnp.where(lane < HEAD_DIM, q, zero), jnp.where(lane >= HEAD_DIM, q, zero)],
                                  axis=0))
    m_ref[...] = jnp.full(m_ref.shape, NEG, F32)
    l_ref[...] = jnp.zeros(l_ref.shape, F32)
    acc_ref[...] = jnp.zeros(acc_ref.shape, F32)

    def scores(hh, j):
        start = pl.multiple_of(j * tile, tile)
        return lax.dot_general(k_ref[pl.ds(start, tile), hh * V_DIM:(hh + 1) * V_DIM], q2[hh],
                               (((1,), (1,)), ((), ())), preferred_element_type=F32)

    def update(hh, j, s, bias, far):
        start = pl.multiple_of(j * tile, tile)
        if bias is not None:
            s = s + bias
        smax = jnp.max(s, axis=0, keepdims=True)
        if far is not None:
            smax = smax + far
        m = m_ref[hh]
        m_new = jnp.maximum(m, smax)
        alpha = jnp.exp2(m - m_new)
        p = jnp.exp2(s - (m_new if far is None else m_new - far))
        l_ref[hh] = alpha * l_ref[hh] + jnp.sum(p, axis=0, keepdims=True)
        acc_ref[hh] = alpha * acc_ref[hh] + _dot(vt_ref[hh, :, pl.ds(start, tile)], p.astype(BF16))
        m_ref[hh] = m_new

    def blocks(j, kind):
        s = scores(0, j)
        for hh in range(heads):
            s_next = scores(hh + 1, j) if hh + 1 < heads else None
            if kind == 2:
                update(hh, j, s, None, bias_ref[hh, 2, 0:1, :])
            else:
                update(hh, j, s, bias_ref[hh, kind], None)
            s = s_next

    def far_body(j, c):
        blocks(j, 2)
        return c

    lax.fori_loop(0, jnp.maximum(i - 1, 0), far_body, 0)

    @pl.when(i >= 1)
    def _():
        blocks(i - 1, 1)

    blocks(i, 0)
    lam = _lambda(lq1, lk1, lq2, lk2, lambda_init)
    for hh in range(heads):
        o = acc_ref[hh] * (1.0 / l_ref[hh])
        od = o[:, :tile] - lam * o[:, tile:]
        ms = jnp.mean(od * od, axis=0, keepdims=True)
        on = od * lax.rsqrt(ms + EPS) * sn_ref[...]
        o_ref[:, hh * V_DIM:(hh + 1) * V_DIM] = on.T.astype(BF16)


def _prompt_attention(q, k, vt, bias, sn, lq1, lk1, lq2, lk2, *, batch, seq_len, lambda_init):
    tile = ATTN_TILE
    hp = ATTN_HEADS_PER_STEP
    nq = seq_len // tile
    assert seq_len % tile == 0 and N_HEADS % hp == 0
    small = pl.BlockSpec((1, HEAD_DIM), lambda b, h, i: (0, 0))
    return pl.pallas_call(
        functools.partial(_flash_kernel, tile=tile, lambda_init=lambda_init),
        grid=(batch, N_HEADS // hp, nq),
        in_specs=[
            pl.BlockSpec((tile, hp * V_DIM), lambda b, h, i: (b * nq + i, h)),
            pl.BlockSpec((seq_len, hp * V_DIM), lambda b, h, i: (b, h)),
            pl.BlockSpec((None, hp, V_DIM, seq_len), lambda b, h, i: (b, h, 0, 0)),
            pl.BlockSpec((hp, 3, tile, 2 * tile), lambda b, h, i: (h, 0, 0, 0)),
            pl.BlockSpec((V_DIM, tile), lambda b, h, i: (0, 0)),
            small, small, small, small,
        ],
        out_specs=pl.BlockSpec((tile, hp * V_DIM), lambda b, h, i: (b * nq + i, h)),
        out_shape=jax.ShapeDtypeStruct(q.shape, BF16),
        scratch_shapes=[pltpu.VMEM((hp, 1, 2 * tile), F32), pltpu.VMEM((hp, 1, 2 * tile), F32),
                        pltpu.VMEM((hp, V_DIM, 2 * tile), F32)],
        compiler_params=_cparams("arbitrary", "arbitrary", "arbitrary"),
        name="prompt_attention",
    )(q, k, vt, bias, sn, lq1, lk1, lq2, lk2)


def _paged_kernel(*refs, n_pages, t_new, lambda_init):
    k_refs = refs[1:1 + n_pages]
    v_refs = refs[1 + n_pages:1 + 2 * n_pages]
    (q_ref, bias_ref, kn_ref, vn_ref, bnew_ref, sn_ref, lq1, lk1, lq2, lk2,
     o_ref, m_ref, l_ref, acc_ref) = refs[1 + 2 * n_pages:]
    step = pl.program_id(1)
    rph = 2 * t_new
    rows = N_HEADS * rph
    cols = k_refs[0].shape[0]

    r = lax.broadcasted_iota(jnp.int32, (rows, V_DIM), 0)
    lane = lax.broadcasted_iota(jnp.int32, (rows, V_DIM), 1)
    own = (lane // HEAD_DIM) == lax.rem(r // t_new, 2)
    q = jnp.where(own, q_ref[...], jnp.zeros((rows, V_DIM), BF16))

    @pl.when(step == 0)
    def _():
        m_ref[...] = jnp.full(m_ref.shape, NEG, F32)
        l_ref[...] = jnp.zeros(l_ref.shape, F32)
        acc_ref[...] = jnp.zeros(acc_ref.shape, F32)

    def head_rows(x, h):
        return x[h * rph:(h + 1) * rph, :]

    def update(s, pv_fn):
        m = m_ref[...]
        m_new = jnp.maximum(m, jnp.max(s, axis=-1, keepdims=True))
        alpha = jnp.exp2(m - m_new)
        p = jnp.exp2(s - m_new)
        l_ref[...] = alpha * l_ref[...] + jnp.sum(p, axis=-1, keepdims=True)
        acc_ref[...] = alpha * acc_ref[...] + pv_fn(p.astype(BF16))
        m_ref[...] = m_new

    def scores(qh, kb):
        return lax.dot_general(qh, kb, (((1,), (1,)), ((), ())), preferred_element_type=F32)

    last = step == pl.num_programs(1) - 1
    parts = []
    for c, kr in enumerate(k_refs):
        bias = bias_ref[0]
        if c == n_pages - 1:
            bias = jnp.where(last, bias_ref[1], bias)
        parts.append(scores(q, kr[...].astype(BF16)) + bias)
    s = jnp.concatenate(parts, axis=1)

    def pv_pages(p):
        o = None
        for c, vr in enumerate(v_refs):
            t = _dot(p[:, c * cols:(c + 1) * cols], vr[...].astype(BF16))
            o = t if o is None else o + t
        return o

    update(s, pv_pages)

    @pl.when(step == pl.num_programs(1) - 1)
    def _():
        lanes = lambda ref, h: ref[:, h * V_DIM:(h + 1) * V_DIM]
        s_new = jnp.concatenate([scores(head_rows(q, h), lanes(kn_ref, h)) for h in range(N_HEADS)], axis=0)
        update(s_new + bnew_ref[...],
               lambda p: jnp.concatenate([_dot(head_rows(p, h), lanes(vn_ref, h)) for h in range(N_HEADS)], axis=0))
        a = acc_ref[...] * (1.0 / l_ref[...])
        a2 = pltpu.roll(a, rows - t_new, axis=0)
        dm = a - _lambda(lq1, lk1, lq2, lk2, lambda_init) * a2
        ms = jnp.mean(dm * dm, axis=-1, keepdims=True)
        o_ref[...] = (dm * lax.rsqrt(ms + EPS) * sn_ref[...]).astype(BF16)


def _sample_attention(q_rows, cache_k, cache_v, page_table, bias, k_new, v_new, bias_new, sn,
                      lq1, lk1, lq2, lk2, *, layer, t_new, lambda_init):
    db, n_tbl = page_table.shape
    depth, n_pool, page, n_heads, v_dim = cache_k.shape
    d = n_heads * v_dim
    rows = N_HEADS * 2 * t_new
    npg = PAGES_PER_STEP
    assert n_tbl % npg == 0 and 2 * t_new == SUBLANES
    assert n_heads == N_HEADS == SUBLANES and v_dim == V_DIM
    cache_k = cache_k.reshape(depth, n_pool, page * n_heads, v_dim)
    cache_v = cache_v.reshape(depth, n_pool, page * n_heads, v_dim)

    def page_spec(c):
        return pl.BlockSpec((None, None, page * n_heads, v_dim),
                            lambda b, s, pt: (layer, pt[b, s * npg + c], 0, 0))

    small = pl.BlockSpec((1, HEAD_DIM), lambda b, s, pt: (0, 0))
    in_specs = [page_spec(c) for c in range(npg)] * 2 + [
        pl.BlockSpec((None, rows, v_dim), lambda b, s, pt: (b, 0, 0)),
        pl.BlockSpec(bias.shape, lambda b, s, pt: (0, 0, 0)),
        pl.BlockSpec((None, SUBLANES, d), lambda b, s, pt: (b, 0, 0)),
        pl.BlockSpec((None, SUBLANES, d), lambda b, s, pt: (b, 0, 0)),
        pl.BlockSpec((rows, SUBLANES), lambda b, s, pt: (0, 0)),
        pl.BlockSpec((1, v_dim), lambda b, s, pt: (0, 0)),
        small, small, small, small,
    ]
    grid_spec = pltpu.PrefetchScalarGridSpec(
        num_scalar_prefetch=1,
        grid=(db, n_tbl // npg),
        in_specs=in_specs,
        out_specs=pl.BlockSpec((None, rows, v_dim), lambda b, s, pt: (b, 0, 0)),
        scratch_shapes=[pltpu.VMEM((rows, 1), F32), pltpu.VMEM((rows, 1), F32), pltpu.VMEM((rows, v_dim), F32)],
    )
    return pl.pallas_call(
        functools.partial(_paged_kernel, n_pages=npg, t_new=t_new, lambda_init=lambda_init),
        grid_spec=grid_spec,
        out_shape=jax.ShapeDtypeStruct((db, rows, v_dim), BF16),
        compiler_params=_cparams("arbitrary", "arbitrary"),
        name="sample_attention",
    )(page_table, *([cache_k] * npg), *([cache_v] * npg), q_rows, bias, k_new, v_new, bias_new, sn,
      lq1, lk1, lq2, lk2)


def _ssm_param_kernel(are_ref, aim_ref, ldt_ref, bre_ref, bim_ref, lre_ref, lim_ref, bbre_ref, bbim_ref):
    ar, ai = are_ref[...], aim_ref[...]
    dt = jnp.exp(ldt_ref[...])
    mag = jnp.exp(ar * dt)
    lr = mag * jnp.cos(ai * dt)
    li = mag * jnp.sin(ai * dt)
    lre_ref[...] = lr
    lim_ref[...] = li
    nr, ni = lr - 1.0, li
    den = ar * ar + ai * ai
    fr = ((nr * ar + ni * ai) / den)[:, None, :]
    fi = ((ni * ar - nr * ai) / den)[:, None, :]
    br, bi = bre_ref[...], bim_ref[...]
    bbre_ref[...] = fr * br - fi * bi
    bbim_ref[...] = fr * bi + fi * br


def _ssm_params(a_re, a_im, log_dt, b_re_t, b_im_t):
    g, p = a_re.shape
    c = b_re_t.shape[1]
    return pl.pallas_call(
        _ssm_param_kernel,
        out_shape=(jax.ShapeDtypeStruct((g, p), F32), jax.ShapeDtypeStruct((g, p), F32),
                   jax.ShapeDtypeStruct((g, c, p), F32), jax.ShapeDtypeStruct((g, c, p), F32)),
        name="ssm_params",
    )(a_re, a_im, log_dt.reshape(g, 1), b_re_t, b_im_t)


def _ssm_kernel(u_ref, h0_ref, lre_ref, lim_ref, bre_ref, bim_ref, cre_ref, cim_ref, d_ref,
                y_ref, hfin_ref, hbuf, carry, *, batch, state_w):
    rc = u_ref.shape[0]
    sw = state_w

    @pl.when(pl.program_id(0) == 0)
    def _():
        carry[...] = h0_ref[...]

    u = u_ref[...]
    ub = u.astype(BF16)
    n_in = bre_ref.shape[0]
    kin, nout = bre_ref.shape[1], bre_ref.shape[2]
    for j in range(n_in):
        uj = ub[:, j * kin:(j + 1) * kin]
        hbuf[:, j * nout:(j + 1) * nout] = _dot(uj, bre_ref[j])
        hbuf[:, sw + j * nout:sw + (j + 1) * nout] = _dot(uj, bim_ref[j])

    rows = carry.shape[0]
    lr = jnp.broadcast_to(lre_ref[...], (rows, sw))
    li = jnp.broadcast_to(lim_ref[...], (rows, sw))

    def step(cr, ci, xr, xi):
        return lr * cr - li * ci + xr, lr * ci + li * cr + xi

    if batch == rows:
        def body(t, c):
            cr, ci = c
            r0 = pl.multiple_of(t * rows, rows)
            hr, hi = step(cr, ci, hbuf[pl.ds(r0, rows), :sw], hbuf[pl.ds(r0, rows), sw:])
            hbuf[pl.ds(r0, rows), :sw] = hr
            hbuf[pl.ds(r0, rows), sw:] = hi
            return hr, hi
    else:
        assert 2 * batch == rows
        upper = lax.broadcasted_iota(jnp.int32, (rows, sw), 0) >= batch

        def body(t, c):
            cr, ci = c
            r0 = pl.multiple_of(t * rows, rows)
            xr, xi = hbuf[pl.ds(r0, rows), :sw], hbuf[pl.ds(r0, rows), sw:]
            ar, ai = step(pltpu.roll(cr, batch, axis=0), pltpu.roll(ci, batch, axis=0), xr, xi)
            br, bi = step(pltpu.roll(ar, batch, axis=0), pltpu.roll(ai, batch, axis=0), xr, xi)
            hr, hi = jnp.where(upper, br, ar), jnp.where(upper, bi, ai)
            hbuf[pl.ds(r0, rows), :sw] = hr
            hbuf[pl.ds(r0, rows), sw:] = hi
            return hr, hi

    cr, ci = lax.fori_loop(0, rc // rows, body, (carry[:, :sw], carry[:, sw:]))
    carry[:, :sw] = cr
    carry[:, sw:] = ci
    hfin_ref[:, :sw] = cr
    hfin_ref[:, sw:] = ci

    n_out = cre_ref.shape[0]
    kout, wout = cre_ref.shape[1], cre_ref.shape[2]
    ys = []
    for o in range(n_out):
        hr = hbuf[:, o * kout:(o + 1) * kout].astype(BF16)
        hi = hbuf[:, sw + o * kout:sw + (o + 1) * kout].astype(BF16)
        ys.append(_dot(hr, cre_ref[o]) + _dot(hi, cim_ref[o]))
    y = jnp.concatenate(ys, axis=1) + d_ref[...] * u
    y_ref[...] = jax.nn.gelu(y).astype(BF16)
    del wout


def _ssm(u_tb, h0, lre, lim, bre, bim, cre, cim, dskip, *, batch):
    n, ch = u_tb.shape
    sw = lre.shape[1]
    rows = h0.shape[0]
    rc = min(SSM_CHUNK_ROWS, n)
    assert n % rc == 0 and rc % rows == 0
    full = lambda a: pl.BlockSpec(a.shape, lambda i: (0,) * a.ndim)
    return pl.pallas_call(
        functools.partial(_ssm_kernel, batch=batch, state_w=sw),
        grid=(n // rc,),
        in_specs=[pl.BlockSpec((rc, ch), lambda i: (i, 0)), full(h0), full(lre), full(lim),
                  full(bre), full(bim), full(cre), full(cim), full(dskip)],
        out_specs=(pl.BlockSpec((rc, ch), lambda i: (i, 0)), full(h0)),
        out_shape=(jax.ShapeDtypeStruct((n, ch), BF16), jax.ShapeDtypeStruct(h0.shape, F32)),
        scratch_shapes=[pltpu.VMEM((rc, 2 * sw), F32), pltpu.VMEM(h0.shape, F32)],
        compiler_params=_cparams("arbitrary"),
        name="ssm",
    )(u_tb, h0, lre, lim, bre, bim, cre, cim, dskip)


def _merge_kernel(x_ref, o_ref, ys_ref, cbz_ref, g_ref, wao_ref, wga_ref, wgb_ref, wco_ref, wo_ref, out_ref):
    d = x_ref.shape[1]
    ys = ys_ref[...]
    attn_out = _dot(o_ref[...], wao_ref[...])
    ssm_out = _dot(ys, wga_ref[...]) * jax.nn.sigmoid(_dot(ys, wgb_ref[...]))
    conv_out = _dot(cbz_ref[...], wco_ref[...])
    merged = (g_ref[:, 0:d] * attn_out + g_ref[:, d:2 * d] * ssm_out + g_ref[:, 2 * d:3 * d] * conv_out)
    out_ref[...] = x_ref[...] + _dot(merged.astype(BF16), wo_ref[...])


def _merge(x, o, ys, cbz, g, wao, wga, wgb, wco, wo):
    n, d = x.shape
    tm = min(ROW_TILE, n)
    row = lambda a: pl.BlockSpec((tm, a.shape[1]), lambda i: (i, 0))
    return pl.pallas_call(
        _merge_kernel,
        grid=(n // tm,),
        in_specs=[row(x), row(o), row(ys), row(cbz), row(g)] + [_resident()] * 5,
        out_specs=row(x),
        out_shape=jax.ShapeDtypeStruct((n, d), F32),
        compiler_params=_cparams("parallel"),
        name="merge",
    )(x, o, ys, cbz, g, wao, wga, wgb, wco, wo)


def _ffn_kernel(x_ref, pe_ref, gffn_ref, wg_ref, wu_ref, wd_ref, gple_ref, wpg_ref, wpp_ref, out_ref):
    x = x_ref[...]
    h = _rms(x, gffn_ref[...]).astype(BF16)
    act = jax.nn.silu(_dot(h, wg_ref[...])) * _dot(h, wu_ref[...])
    x = x + _dot(act.astype(BF16), wd_ref[...])
    h = _rms(x, gple_ref[...]).astype(BF16)
    gate = jax.nn.sigmoid(_dot(h, wpg_ref[...]))
    out_ref[...] = x + gate * _dot(pe_ref[...].astype(BF16), wpp_ref[...])


def _ffn(x, pe, gffn, wg, wu, wd, gple, wpg, wpp):
    n, d = x.shape
    tm = min(ROW_TILE, n)
    row = lambda a: pl.BlockSpec((tm, a.shape[1]), lambda i: (i, 0))
    full = lambda a: pl.BlockSpec(a.shape, lambda i: (0,) * a.ndim)
    return pl.pallas_call(
        _ffn_kernel,
        grid=(n // tm,),
        in_specs=[row(x), row(pe), full(gffn), _resident(), _resident(), _resident(), full(gple),
                  _resident(), _resident()],
        out_specs=row(x),
        out_shape=jax.ShapeDtypeStruct((n, d), F32),
        compiler_params=_cparams("parallel"),
        name="ffn",
    )(x, pe, gffn, wg, wu, wd, gple, wpg, wpp)


def _block_diag(blocks):
    n, r, c = blocks.shape
    eye = jnp.eye(n, dtype=bool)
    return jnp.where(eye[:, None, :, None], blocks[:, :, None, :], 0).reshape(n * r, n * c)


def _ssm_matrices(lw):
    g, p = lw["a_re"].shape
    lre, lim, bbre, bbim = _ssm_params(lw["a_re"], lw["a_im"], lw["log_dt"],
                                       jnp.swapaxes(lw["b_re"], 1, 2), jnp.swapaxes(lw["b_im"], 1, 2))
    c = bbre.shape[1]
    gin = 128 // c
    b_in = lambda bb: jax.vmap(_block_diag)(bb.reshape(g // gin, gin, c, p)).astype(BF16)
    gout = MXU_DIM // c
    c_out = lambda cc: jax.vmap(_block_diag)(
        jnp.swapaxes(cc, 1, 2).reshape(g // gout, gout, p, c)).astype(BF16)
    return (lre.reshape(1, g * p), lim.reshape(1, g * p), b_in(bbre), b_in(bbim),
            c_out(lw["c_re"]), c_out(-lw["c_im"]))


def _layer(x, pe, attend, h0_re, h0_im, conv_state, lw, mats, lambda_init, *, batch, seq_len):
    n, d = x.shape
    outs = _in_proj(x, lw["g_mix"], lw["w_in"], lw["qn"], lw["kn"], lw["seg"], lw["w_conv"], conv_state,
                    seq_len=seq_len)
    q, k_f, k_b, v_f, v_b, u, z, cbz, gates = outs
    o = attend(q, k_b, v_b)

    lre, lim, bre, bim, cre, cim = mats
    sw = lre.shape[1]
    ch = u.shape[1]
    u_tb = u.reshape(batch, seq_len, ch).swapaxes(0, 1).reshape(n, ch)
    h0 = jnp.concatenate([h0_re.reshape(batch, sw), h0_im.reshape(batch, sw)], axis=1)
    rows = max(batch, SUBLANES)
    if batch < rows:
        h0 = jnp.concatenate([jnp.zeros((rows - batch, 2 * sw), F32), h0], axis=0)
    ys_tb, hfin = _ssm(u_tb, h0, lre, lim, bre, bim, cre, cim, lw["d_skip"], batch=batch)
    ys = ys_tb.reshape(seq_len, batch, ch).swapaxes(0, 1).reshape(n, ch)
    hfin = hfin[rows - batch:]
    groups = sw // SSM_STATE

    x1 = _merge(x, o, ys, cbz, gates, lw["w_attn_out"], lw["w_glu_a"], lw["w_glu_b"], lw["w_conv_out"], lw["w_o"])
    x2 = _ffn(x1, pe, lw["g_ffn"], lw["w_ffn_gate"], lw["w_ffn_up"], lw["w_ffn_down"], lw["g_ple"],
              lw["w_ple_gate"], lw["w_ple_proj"])
    return (x2,
            k_f.reshape(batch, seq_len, N_HEADS, V_DIM),
            v_f.reshape(batch, seq_len, N_HEADS, V_DIM),
            hfin[:, :sw].reshape(batch, groups, SSM_STATE),
            hfin[:, sw:].reshape(batch, groups, SSM_STATE),
            z.reshape(batch, seq_len, ch)[:, seq_len - (CONV_W - 1):],
            k_b, v_b)


def kernel(x_prompt, x_sample, cache_k, cache_v, state_ssm_re, state_ssm_im, state_conv, page_table, p_prompt, p_sample, rel_table, g_mix, w_in, q_norm, k_norm, lambda_q1, lambda_k1, lambda_q2, lambda_k2, sub_norm, w_attn_out, ssm_a_re, ssm_a_im, ssm_log_dt, ssm_b_re, ssm_b_im, ssm_c_re, ssm_c_im, ssm_d, w_glu_a, w_glu_b, w_conv, w_conv_out, w_o, g_ffn, w_ffn_gate, w_ffn_up, w_ffn_down, g_ple, w_ple_gate, w_ple_proj):
    bp, seq, d = x_prompt.shape
    db, t_new, _ = x_sample.shape
    depth = w_in.shape[0]
    page = cache_k.shape[2]
    past = page_table.shape[1] * page
    conv_dim = w_conv.shape[2]

    bias_p = _prompt_bias(rel_table, ATTN_TILE)
    bias_s, bias_new = _sample_bias(rel_table, past, t_new, page)
    seg = (_block_diag(jnp.ones((MXU_DIM // HEAD_DIM, HEAD_DIM, HEAD_DIM), F32)) / HEAD_DIM).astype(BF16)
    reps = d // HEAD_DIM

    xp = x_prompt.reshape(bp * seq, d)
    xs = x_sample.reshape(db * t_new, d)
    outs_p, outs_s = [], []
    for i in range(depth):
        lambda_init = 0.8 - 0.6 * math.exp(-0.3 * i)
        bf = lambda w: w[i].astype(BF16)
        vec = lambda w: w[i].reshape(1, -1)
        lw = dict(
            g_mix=vec(g_mix), w_in=bf(w_in), w_conv=w_conv[i], seg=seg,
            qn=jnp.tile(q_norm[i], reps).reshape(1, d) * (HEAD_DIM ** -0.5 * LOG2E),
            kn=jnp.tile(k_norm[i], reps).reshape(1, d),
            w_attn_out=bf(w_attn_out), w_glu_a=bf(w_glu_a), w_glu_b=bf(w_glu_b),
            w_conv_out=bf(w_conv_out), w_o=bf(w_o), g_ffn=vec(g_ffn), w_ffn_gate=bf(w_ffn_gate),
            w_ffn_up=bf(w_ffn_up), w_ffn_down=bf(w_ffn_down), g_ple=vec(g_ple),
            w_ple_gate=bf(w_ple_gate), w_ple_proj=bf(w_ple_proj),
            a_re=ssm_a_re[i], a_im=ssm_a_im[i], log_dt=ssm_log_dt[i], b_re=ssm_b_re[i], b_im=ssm_b_im[i],
            c_re=ssm_c_re[i], c_im=ssm_c_im[i], d_skip=vec(ssm_d),
        )
        mats = _ssm_matrices(lw)
        lam_args = tuple(w[i].reshape(1, HEAD_DIM) for w in (lambda_q1, lambda_k1, lambda_q2, lambda_k2))
        sn_scaled = sub_norm[i] * (1.0 - lambda_init)

        def attend_prompt(q, k_b, v_b):
            vt = v_b.reshape(bp, seq, N_HEADS, V_DIM).transpose(0, 2, 3, 1)
            sn = jnp.broadcast_to(sn_scaled[:, None], (V_DIM, ATTN_TILE))
            return _prompt_attention(q, k_b, vt, bias_p, sn, *lam_args, batch=bp, seq_len=seq,
                                     lambda_init=lambda_init)

        def attend_sample(q, k_b, v_b):
            qh = q.reshape(db, t_new, N_HEADS, V_DIM).transpose(0, 2, 1, 3)
            q_rows = jnp.broadcast_to(qh[:, :, None], (db, N_HEADS, 2, t_new, V_DIM)).reshape(db, -1, V_DIM)
            pad = lambda a: jnp.pad(a.reshape(db, t_new, d), ((0, 0), (0, SUBLANES - t_new), (0, 0)))
            o = _sample_attention(q_rows, cache_k, cache_v, page_table, bias_s, pad(k_b), pad(v_b),
                                  bias_new, sn_scaled.reshape(1, V_DIM), *lam_args, layer=i, t_new=t_new,
                                  lambda_init=lambda_init)
            o = o.reshape(db, N_HEADS, 2, t_new, V_DIM)[:, :, 0]
            return o.transpose(0, 2, 1, 3).reshape(db * t_new, d)

        zs = jnp.zeros((bp, d // 2 // SSM_GROUP * SSM_STATE), F32)
        rp = _layer(xp, p_prompt[i].reshape(bp * seq, -1), attend_prompt, zs, zs,
                    jnp.zeros((bp, CONV_W - 1, conv_dim), F32), lw, mats, lambda_init, batch=bp, seq_len=seq)
        rs = _layer(xs, p_sample[i].reshape(db * t_new, -1), attend_sample, state_ssm_re[i], state_ssm_im[i],
                    state_conv[i], lw, mats, lambda_init, batch=db, seq_len=t_new)
        xp, xs = rp[0], rs[0]
        outs_p.append(rp[1:6])
        outs_s.append(rs[1:6])

    stack = lambda outs, j: jnp.stack([o[j] for o in outs])
    return ((xp.reshape(bp, seq, d), xs.reshape(db, t_new, d))
            + tuple(stack(outs_p, j) for j in range(5))
            + tuple(stack(outs_s, j) for j in range(5)))
```

```python
import functools
import math

import jax
import jax.numpy as jnp
from jax import lax
from jax.experimental import pallas as pl
from jax.experimental.pallas import tpu as pltpu

F32 = jnp.float32
BF16 = jnp.bfloat16

N_HEADS = 8
HEAD_DIM = 64
V_DIM = 2 * HEAD_DIM
SSM_GROUP = 16
SSM_STATE = 64
CONV_W = 3
N_BRANCH = 3
REL_BUCKETS = 32
REL_MAX_DIST = 128
EPS = 1e-6
NEG = -1e30
LOG2E = 1.4426950408889634

V7X_VMEM_BYTES = 64 * 1024 * 1024
VMEM_LIMIT_BYTES = (V7X_VMEM_BYTES * 7) // 8
MXU_DIM = 256
SUBLANES = 8
ROW_TILE = 256
ATTN_TILE = 256
ATTN_HEADS_PER_STEP = 4
BIAS_FAR = 2
BIAS_MASKED = 3
SSM_CHUNK_ROWS = 256
PAGES_PER_STEP = 16
PAGE_GROUP = 2


def _cparams(*sem):
    return pltpu.CompilerParams(dimension_semantics=sem, vmem_limit_bytes=VMEM_LIMIT_BYTES)


def _resident():
    return pl.BlockSpec(memory_space=pltpu.VMEM)


def _rms(x, g):
    ms = jnp.mean(x * x, axis=-1, keepdims=True)
    return x * lax.rsqrt(ms + EPS) * g


def _dot(a, b):
    return jnp.dot(a, b, preferred_element_type=F32)


def _head_norm(acc, seg, gain):
    outs = []
    for c in range(acc.shape[1] // MXU_DIM):
        a = acc[:, c * MXU_DIM:(c + 1) * MXU_DIM]
        ms = _dot((a * a).astype(BF16), seg)
        outs.append(a * lax.rsqrt(ms + EPS))
    return jnp.concatenate(outs, axis=1) * gain


def _in_proj_kernel(*refs, tm, seq_len, d_model, conv_dim, carry_mode):
    if carry_mode:
        (x_ref, gmix_ref, w_ref, qn_ref, kn_ref, seg_ref, wconv_ref, buf_ref,
         q_ref, kf_ref, kb_ref, vf_ref, vb_ref, u_ref, z_ref, cbz_ref, g_ref, carry_ref) = refs
    else:
        (x_ref, gmix_ref, w_ref, qn_ref, kn_ref, seg_ref, wconv_ref, p1_ref, p2_ref,
         q_ref, kf_ref, kb_ref, vf_ref, vb_ref, u_ref, z_ref, cbz_ref, g_ref) = refs
    d = d_model
    h = _rms(x_ref[...], gmix_ref[...]).astype(BF16)

    def proj(c0, c1):
        return _dot(h, w_ref[:, c0:c1])

    seg = seg_ref[...]
    q_ref[...] = _head_norm(proj(0, d), seg, qn_ref[...]).astype(BF16)
    k = _head_norm(proj(d, 2 * d), seg, kn_ref[...])
    kf_ref[...] = k
    kb_ref[...] = k.astype(BF16)
    v = proj(2 * d, 3 * d)
    vf_ref[...] = v
    vb_ref[...] = v.astype(BF16)
    c0 = 3 * d
    ucb = proj(c0, c0 + 2 * conv_dim)
    u_ref[...] = ucb[:, :conv_dim]
    cb = ucb[:, conv_dim:]
    ccx = proj(c0 + 2 * conv_dim, c0 + 4 * conv_dim)
    z = ccx[:, :conv_dim] * ccx[:, conv_dim:]
    z_ref[...] = z

    row = lax.broadcasted_iota(jnp.int32, z.shape, 0)
    zr1 = pltpu.roll(z, 1, axis=0)
    zr2 = pltpu.roll(z, 2, axis=0)
    if carry_mode:
        tiles_per_seq = seq_len // tm

        @pl.when(pl.program_id(0) % tiles_per_seq == 0)
        def _():
            carry_ref[...] = buf_ref[...]

        c_m2 = carry_ref[SUBLANES - 2:SUBLANES - 1, :]
        c_m1 = carry_ref[SUBLANES - 1:SUBLANES, :]
        z1 = jnp.where(row == 0, c_m1, zr1)
        z2 = jnp.where(row == 0, c_m2, jnp.where(row == 1, c_m1, zr2))
        carry_ref[...] = z[tm - SUBLANES:tm, :]
    else:
        t = lax.rem(row, seq_len)
        z1 = jnp.where(t >= 1, zr1, p1_ref[...])
        z2 = jnp.where(t >= 2, zr2, p2_ref[...])
    zc = wconv_ref[0:1, :] * z2 + wconv_ref[1:2, :] * z1 + wconv_ref[2:3, :] * z
    cbz_ref[...] = (cb * zc).astype(BF16)

    g0 = c0 + 4 * conv_dim
    for c in range(N_BRANCH):
        g_ref[:, c * d:(c + 1) * d] = jax.nn.sigmoid(proj(g0 + c * d, g0 + (c + 1) * d))


def _in_proj(x, gmix, w_in, qn, kn, seg, wconv, conv_state, *, seq_len):
    n, d = x.shape
    conv_dim = wconv.shape[1]
    tm = min(ROW_TILE, n)
    carry_mode = seq_len >= tm
    assert n % tm == 0 and seq_len >= CONV_W - 1
    assert (seq_len % tm == 0) if carry_mode else (tm % seq_len == 0)
    n_seq = n // seq_len
    row = lambda w: pl.BlockSpec((tm, w), lambda i: (i, 0))
    full = lambda a: pl.BlockSpec(a.shape, lambda i: (0,) * a.ndim)
    if carry_mode:
        buf = jnp.zeros((n_seq, SUBLANES, conv_dim), F32).at[:, SUBLANES - 2:, :].set(conv_state)
        tiles_per_seq = seq_len // tm
        state_args = (buf,)
        state_specs = [pl.BlockSpec((None, SUBLANES, conv_dim), lambda i: (i // tiles_per_seq, 0, 0))]
        scratch = [pltpu.VMEM((SUBLANES, conv_dim), F32)]
    else:
        zeros = jnp.zeros((n_seq, seq_len, conv_dim), F32)
        p1 = zeros.at[:, 0].set(conv_state[:, 1]).reshape(n, conv_dim)
        p2 = zeros.at[:, 0].set(conv_state[:, 0]).at[:, 1].set(conv_state[:, 1]).reshape(n, conv_dim)
        state_args = (p1, p2)
        state_specs = [row(conv_dim), row(conv_dim)]
        scratch = []
    out_shape = (
        jax.ShapeDtypeStruct((n, d), BF16),
        jax.ShapeDtypeStruct((n, d), F32),
        jax.ShapeDtypeStruct((n, d), BF16),
        jax.ShapeDtypeStruct((n, d), F32),
        jax.ShapeDtypeStruct((n, d), BF16),
        jax.ShapeDtypeStruct((n, conv_dim), F32),
        jax.ShapeDtypeStruct((n, conv_dim), F32),
        jax.ShapeDtypeStruct((n, conv_dim), BF16),
        jax.ShapeDtypeStruct((n, N_BRANCH * d), F32),
    )
    out_specs = (row(d), row(d), row(d), row(d), row(d), row(conv_dim), row(conv_dim),
                 row(conv_dim), row(N_BRANCH * d))
    return pl.pallas_call(
        functools.partial(_in_proj_kernel, tm=tm, seq_len=seq_len, d_model=d, conv_dim=conv_dim,
                          carry_mode=carry_mode),
        grid=(n // tm,),
        in_specs=[row(d), full(gmix), _resident(), full(qn), full(kn), full(seg), full(wconv)] + state_specs,
        out_specs=out_specs,
        out_shape=out_shape,
        scratch_shapes=scratch,
        compiler_params=_cparams("arbitrary"),
        name="in_proj",
    )(x, gmix, w_in, qn, kn, seg, wconv, *state_args)


def _rel_bias(n, tbl_ref, head):
    max_exact = REL_BUCKETS // 2
    nc = jnp.maximum(n, 0)
    nf = jnp.maximum(nc, max_exact).astype(F32)
    large = max_exact + (jnp.log(nf / max_exact) / math.log(REL_MAX_DIST / max_exact)
                         * (REL_BUCKETS - max_exact)).astype(jnp.int32)
    large = jnp.minimum(large, REL_BUCKETS - 1)
    bucket = jnp.where(nc < max_exact, nc, large)
    val = jnp.zeros(n.shape, F32)
    for b in range(REL_BUCKETS):
        val = jnp.where(bucket == b, tbl_ref[b, head], val)
    return jnp.where(n >= 0, val * LOG2E, NEG)


def _prompt_bias_kernel(tbl_ref, o_ref, *, tile):
    head, d = pl.program_id(0), pl.program_id(1)
    key = lax.broadcasted_iota(jnp.int32, (tile, tile), 0)
    qry = lax.broadcasted_iota(jnp.int32, (tile, tile), 1)
    n = jnp.where(d == BIAS_MASKED, -1, d * tile + qry - key)
    b = _rel_bias(n, tbl_ref, head)
    o_ref[...] = jnp.concatenate([b, b], axis=1)


def _prompt_bias(rel_table, tile):
    assert tile >= REL_MAX_DIST
    return pl.pallas_call(
        functools.partial(_prompt_bias_kernel, tile=tile),
        grid=(N_HEADS, BIAS_MASKED + 1),
        in_specs=[pl.BlockSpec(memory_space=pltpu.SMEM)],
        out_specs=pl.BlockSpec((None, None, tile, 2 * tile), lambda h, d: (h, d, 0, 0)),
        out_shape=jax.ShapeDtypeStruct((N_HEADS, BIAS_MASKED + 1, tile, 2 * tile), F32),
        compiler_params=_cparams("arbitrary", "arbitrary"),
        name="prompt_bias",
    )(rel_table)


def _sample_bias_kernel(tbl_ref, o_ref, onew_ref, *, past, t_new, page):
    shape = (N_HEADS * 2 * t_new, page * N_HEADS)
    r = lax.broadcasted_iota(jnp.int32, shape, 0)
    col = lax.broadcasted_iota(jnp.int32, shape, 1)
    first_key = jnp.where(pl.program_id(0) == 0, 0, past - page)
    n = past + lax.rem(r, t_new) - (first_key + col // N_HEADS)
    rn = lax.broadcasted_iota(jnp.int32, onew_ref.shape, 0)
    jn = lax.broadcasted_iota(jnp.int32, onew_ref.shape, 1)
    nn = jnp.where(jn < t_new, lax.rem(rn, t_new) - jn, -1)
    acc = jnp.full(shape, NEG, F32)
    accn = jnp.zeros(onew_ref.shape, F32)
    rows_per_head = 2 * t_new
    for head in range(N_HEADS):
        own = (r // rows_per_head == head) & (lax.rem(col, N_HEADS) == head)
        acc = jnp.where(own, _rel_bias(n, tbl_ref, head), acc)
        accn = jnp.where(rn // rows_per_head == head, _rel_bias(nn, tbl_ref, head), accn)
    o_ref[...] = acc
    onew_ref[...] = accn


def _sample_bias(rel_table, past, t_new, page):
    assert page >= REL_MAX_DIST and past >= 2 * page
    rows = N_HEADS * 2 * t_new
    return pl.pallas_call(
        functools.partial(_sample_bias_kernel, past=past, t_new=t_new, page=page),
        grid=(2,),
        in_specs=[pl.BlockSpec(memory_space=pltpu.SMEM)],
        out_specs=(pl.BlockSpec((None, rows, page * N_HEADS), lambda s: (s, 0, 0)),
                   pl.BlockSpec((rows, SUBLANES), lambda s: (0, 0))),
        out_shape=(jax.ShapeDtypeStruct((2, rows, page * N_HEADS), F32),
                   jax.ShapeDtypeStruct((rows, SUBLANES), F32)),
        compiler_params=_cparams("arbitrary"),
        name="sample_bias",
    )(rel_table)


def _lambda(lq1, lk1, lq2, lk2, lambda_init):
    s1 = jnp.sum(lq1[...] * lk1[...], axis=-1, keepdims=True)
    s2 = jnp.sum(lq2[...] * lk2[...], axis=-1, keepdims=True)
    return jnp.exp(s1) - jnp.exp(s2) + lambda_init


def _flash_kernel(q_ref, k_ref, vt_ref, bias_ref, sn_ref, lq1, lk1, lq2, lk2, o_ref, m_ref, l_ref, acc_ref,
                  s_ref, *, tile, lambda_init):
    i = pl.program_id(2)
    heads = m_ref.shape[0]
    lane = lax.broadcasted_iota(jnp.int32, (tile, V_DIM), 1)
    zero = jnp.zeros((tile, V_DIM), BF16)
    q2 = []
    for hh in range(heads):
        q = q_ref[:, hh * V_DIM:(hh + 1) * V_DIM]
        q2.append(jnp.concatenate([jnp.where(lane < HEAD_DIM, q, zero), jnp.where(lane >= HEAD_DIM, q, zero)],
                                  axis=0))
    m_ref[...] = jnp.full(m_ref.shape, NEG, F32)
    l_ref[...] = jnp.zeros(l_ref.shape, F32)
    acc_ref[...] = jnp.zeros(acc_ref.shape, F32)

    def key_start(j):
        return pl.multiple_of(jnp.minimum(j, i) * tile, tile)

    def scores(hh, j):
        return lax.dot_general(k_ref[pl.ds(key_start(j), tile), hh * V_DIM:(hh + 1) * V_DIM], q2[hh],
                               (((1,), (1,)), ((), ())), preferred_element_type=F32)

    def update(hh, j, s, bias, far):
        if bias is not None:
            s = s + bias
        smax = jnp.max(s, axis=0, keepdims=True)
        if far is not None:
            smax = smax + far
        m = m_ref[hh]
        m_new = jnp.maximum(m, smax)
        alpha = jnp.exp2(m - m_new)
        p = jnp.exp2(s - (m_new if far is None else m_new - far))
        l_ref[hh] = alpha * l_ref[hh] + jnp.sum(p, axis=0, keepdims=True)
        acc_ref[hh] = alpha * acc_ref[hh] + _dot(vt_ref[hh, :, pl.ds(key_start(j), tile)], p.astype(BF16))
        m_ref[hh] = m_new

    n_pairs = lax.shift_right_logical(i + 2, 1)
    n_far_pairs = lax.shift_right_logical(jnp.maximum(i - 1, 0), 1)

    def pair(jj, far):
        for cur in (0, 1):
            j = 2 * jj + cur
            kind = jnp.where(j > i, BIAS_MASKED, jnp.minimum(i - j, BIAS_FAR))
            for hh in range(heads):
                s_ref[1 - cur, hh] = scores(hh, j + 1)
                if far:
                    update(hh, j, s_ref[cur, hh], None, bias_ref[hh, BIAS_FAR, 0:1, :])
                else:
                    update(hh, j, s_ref[cur, hh], bias_ref[hh, kind], None)

    for hh in range(heads):
        s_ref[0, hh] = scores(hh, 0)

    def far_body(jj, c):
        pair(jj, True)
        return c

    def near_body(jj, c):
        pair(jj, False)
        return c

    lax.fori_loop(0, n_far_pairs, far_body, 0)
    lax.fori_loop(n_far_pairs, n_pairs, near_body, 0)
    lam = _lambda(lq1, lk1, lq2, lk2, lambda_init)
    for hh in range(heads):
        o = acc_ref[hh] * (1.0 / l_ref[hh])
        od = o[:, :tile] - lam * o[:, tile:]
        ms = jnp.mean(od * od, axis=0, keepdims=True)
        on = od * lax.rsqrt(ms + EPS) * sn_ref[...]
        o_ref[:, hh * V_DIM:(hh + 1) * V_DIM] = on.T.astype(BF16)


def _prompt_attention(q, k, vt, bias, sn, lq1, lk1, lq2, lk2, *, batch, seq_len, lambda_init):
    tile = ATTN_TILE
    hp = ATTN_HEADS_PER_STEP
    nq = seq_len // tile
    assert seq_len % tile == 0 and N_HEADS % hp == 0
    small = pl.BlockSpec((1, HEAD_DIM), lambda b, h, i: (0, 0))
    return pl.pallas_call(
        functools.partial(_flash_kernel, tile=tile, lambda_init=lambda_init),
        grid=(batch, N_HEADS // hp, nq),
        in_specs=[
            pl.BlockSpec((tile, hp * V_DIM), lambda b, h, i: (b * nq + i, h)),
            pl.BlockSpec((seq_len, hp * V_DIM), lambda b, h, i: (b, h)),
            pl.BlockSpec((None, hp, V_DIM, seq_len), lambda b, h, i: (b, h, 0, 0)),
            pl.BlockSpec((hp, BIAS_MASKED + 1, tile, 2 * tile), lambda b, h, i: (h, 0, 0, 0)),
            pl.BlockSpec((V_DIM, tile), lambda b, h, i: (0, 0)),
            small, small, small, small,
        ],
        out_specs=pl.BlockSpec((tile, hp * V_DIM), lambda b, h, i: (b * nq + i, h)),
        out_shape=jax.ShapeDtypeStruct(q.shape, BF16),
        scratch_shapes=[pltpu.VMEM((hp, 1, 2 * tile), F32), pltpu.VMEM((hp, 1, 2 * tile), F32),
                        pltpu.VMEM((hp, V_DIM, 2 * tile), F32), pltpu.VMEM((2, hp, tile, 2 * tile), F32)],
        compiler_params=_cparams("arbitrary", "arbitrary", "arbitrary"),
        name="prompt_attention",
    )(q, k, vt, bias, sn, lq1, lk1, lq2, lk2)


def _paged_kernel(*refs, n_pages, t_new, lambda_init):
    k_refs = refs[1:1 + n_pages]
    v_refs = refs[1 + n_pages:1 + 2 * n_pages]
    (q_ref, bias_ref, kn_ref, vn_ref, bnew_ref, sn_ref, lq1, lk1, lq2, lk2,
     o_ref, m_ref, l_ref, acc_ref) = refs[1 + 2 * n_pages:]
    step = pl.program_id(1)
    rph = 2 * t_new
    rows = N_HEADS * rph
    cols = k_refs[0].shape[0]

    r = lax.broadcasted_iota(jnp.int32, (rows, V_DIM), 0)
    lane = lax.broadcasted_iota(jnp.int32, (rows, V_DIM), 1)
    own = (lane // HEAD_DIM) == lax.rem(r // t_new, 2)
    q = jnp.where(own, q_ref[...], 0.0)

    @pl.when(step == 0)
    def _():
        m_ref[...] = jnp.full(m_ref.shape, NEG, F32)
        l_ref[...] = jnp.zeros(l_ref.shape, F32)
        acc_ref[...] = jnp.zeros(acc_ref.shape, F32)

    def head_rows(x, h):
        return x[h * rph:(h + 1) * rph, :]

    def update(s, pv_fn):
        m = m_ref[...]
        m_new = jnp.maximum(m, jnp.max(s, axis=-1, keepdims=True))
        alpha = jnp.exp2(m - m_new)
        p = jnp.exp2(s - m_new)
        l_ref[...] = alpha * l_ref[...] + jnp.sum(p, axis=-1, keepdims=True)
        acc_ref[...] = alpha * acc_ref[...] + pv_fn(p)
        m_ref[...] = m_new

    def scores(qh, kb):
        return lax.dot_general(qh, kb, (((1,), (1,)), ((), ())), preferred_element_type=F32)

    last = step == pl.num_programs(1) - 1

    def group_scores(g):
        parts = []
        for c in range(g * PAGE_GROUP, (g + 1) * PAGE_GROUP):
            bias = bias_ref[0]
            if c == n_pages - 1:
                bias = jnp.where(last, bias_ref[1], bias)
            parts.append(scores(q, k_refs[c][...]) + bias)
        return jnp.concatenate(parts, axis=1)

    def group_pv(g):
        def pv(p):
            o = None
            for n, c in enumerate(range(g * PAGE_GROUP, (g + 1) * PAGE_GROUP)):
                t = _dot(p[:, n * cols:(n + 1) * cols], v_refs[c][...])
                o = t if o is None else o + t
            return o
        return pv

    n_groups = n_pages // PAGE_GROUP
    s = group_scores(0)
    for g in range(n_groups):
        s_next = group_scores(g + 1) if g + 1 < n_groups else None
        update(s, group_pv(g))
        s = s_next

    @pl.when(step == pl.num_programs(1) - 1)
    def _():
        lanes = lambda ref, h: ref[:, h * V_DIM:(h + 1) * V_DIM]
        s_new = jnp.concatenate([scores(head_rows(q, h), lanes(kn_ref, h)) for h in range(N_HEADS)], axis=0)
        update(s_new + bnew_ref[...],
               lambda p: jnp.concatenate([_dot(head_rows(p, h), lanes(vn_ref, h)) for h in range(N_HEADS)], axis=0))
        a = acc_ref[...] * (1.0 / l_ref[...])
        a2 = pltpu.roll(a, rows - t_new, axis=0)
        dm = a - _lambda(lq1, lk1, lq2, lk2, lambda_init) * a2
        ms = jnp.mean(dm * dm, axis=-1, keepdims=True)
        o_ref[...] = (dm * lax.rsqrt(ms + EPS) * sn_ref[...]).astype(BF16)


def _sample_attention(q_rows, cache_k, cache_v, page_table, bias, k_new, v_new, bias_new, sn,
                      lq1, lk1, lq2, lk2, *, layer, t_new, lambda_init):
    db, n_tbl = page_table.shape
    depth, n_pool, page, n_heads, v_dim = cache_k.shape
    d = n_heads * v_dim
    rows = N_HEADS * 2 * t_new
    npg = PAGES_PER_STEP
    assert n_tbl % npg == 0 and npg % PAGE_GROUP == 0 and 2 * t_new == SUBLANES
    assert n_heads == N_HEADS == SUBLANES and v_dim == V_DIM
    cache_k = cache_k.reshape(depth, n_pool, page * n_heads, v_dim)
    cache_v = cache_v.reshape(depth, n_pool, page * n_heads, v_dim)

    def page_spec(c):
        return pl.BlockSpec((None, None, page * n_heads, v_dim),
                            lambda b, s, pt: (layer, pt[b, s * npg + c], 0, 0))

    small = pl.BlockSpec((1, HEAD_DIM), lambda b, s, pt: (0, 0))
    in_specs = [page_spec(c) for c in range(npg)] * 2 + [
        pl.BlockSpec((None, rows, v_dim), lambda b, s, pt: (b, 0, 0)),
        pl.BlockSpec(bias.shape, lambda b, s, pt: (0, 0, 0)),
        pl.BlockSpec((None, SUBLANES, d), lambda b, s, pt: (b, 0, 0)),
        pl.BlockSpec((None, SUBLANES, d), lambda b, s, pt: (b, 0, 0)),
        pl.BlockSpec((rows, SUBLANES), lambda b, s, pt: (0, 0)),
        pl.BlockSpec((1, v_dim), lambda b, s, pt: (0, 0)),
        small, small, small, small,
    ]
    grid_spec = pltpu.PrefetchScalarGridSpec(
        num_scalar_prefetch=1,
        grid=(db, n_tbl // npg),
        in_specs=in_specs,
        out_specs=pl.BlockSpec((None, rows, v_dim), lambda b, s, pt: (b, 0, 0)),
        scratch_shapes=[pltpu.VMEM((rows, 1), F32), pltpu.VMEM((rows, 1), F32), pltpu.VMEM((rows, v_dim), F32)],
    )
    return pl.pallas_call(
        functools.partial(_paged_kernel, n_pages=npg, t_new=t_new, lambda_init=lambda_init),
        grid_spec=grid_spec,
        out_shape=jax.ShapeDtypeStruct((db, rows, v_dim), BF16),
        compiler_params=_cparams("arbitrary", "arbitrary"),
        name="sample_attention",
    )(page_table, *([cache_k] * npg), *([cache_v] * npg), q_rows, bias, k_new, v_new, bias_new, sn,
      lq1, lk1, lq2, lk2)


def _ssm_param_kernel(are_ref, aim_ref, ldt_ref, bre_ref, bim_ref, lre_ref, lim_ref, bbre_ref, bbim_ref):
    ar, ai = are_ref[...], aim_ref[...]
    dt = jnp.exp(ldt_ref[...])
    mag = jnp.exp(ar * dt)
    lr = mag * jnp.cos(ai * dt)
    li = mag * jnp.sin(ai * dt)
    lre_ref[...] = lr
    lim_ref[...] = li
    nr, ni = lr - 1.0, li
    den = ar * ar + ai * ai
    fr = ((nr * ar + ni * ai) / den)[:, None, :]
    fi = ((ni * ar - nr * ai) / den)[:, None, :]
    br, bi = bre_ref[...], bim_ref[...]
    bbre_ref[...] = fr * br - fi * bi
    bbim_ref[...] = fr * bi + fi * br


def _ssm_params(a_re, a_im, log_dt, b_re_t, b_im_t):
    g, p = a_re.shape
    c = b_re_t.shape[1]
    return pl.pallas_call(
        _ssm_param_kernel,
        out_shape=(jax.ShapeDtypeStruct((g, p), F32), jax.ShapeDtypeStruct((g, p), F32),
                   jax.ShapeDtypeStruct((g, c, p), F32), jax.ShapeDtypeStruct((g, c, p), F32)),
        name="ssm_params",
    )(a_re, a_im, log_dt.reshape(g, 1), b_re_t, b_im_t)


def _ssm_kernel(u_ref, h0_ref, lre_ref, lim_ref, bre_ref, bim_ref, cre_ref, cim_ref, d_ref,
                y_ref, hfin_ref, hbuf, carry, *, batch, state_w):
    rc = u_ref.shape[0]
    sw = state_w

    @pl.when(pl.program_id(0) == 0)
    def _():
        carry[...] = h0_ref[...]

    u = u_ref[...]
    ub = u.astype(BF16)
    n_in = bre_ref.shape[0]
    kin, nout = bre_ref.shape[1], bre_ref.shape[2]
    for j in range(n_in):
        uj = ub[:, j * kin:(j + 1) * kin]
        hbuf[:, j * nout:(j + 1) * nout] = _dot(uj, bre_ref[j])
        hbuf[:, sw + j * nout:sw + (j + 1) * nout] = _dot(uj, bim_ref[j])

    rows = carry.shape[0]
    lr = jnp.broadcast_to(lre_ref[...], (rows, sw))
    li = jnp.broadcast_to(lim_ref[...], (rows, sw))

    def step(cr, ci, xr, xi):
        return lr * cr - li * ci + xr, lr * ci + li * cr + xi

    if batch == rows:
        def body(t, c):
            cr, ci = c
            r0 = pl.multiple_of(t * rows, rows)
            hr, hi = step(cr, ci, hbuf[pl.ds(r0, rows), :sw], hbuf[pl.ds(r0, rows), sw:])
            hbuf[pl.ds(r0, rows), :sw] = hr
            hbuf[pl.ds(r0, rows), sw:] = hi
            return hr, hi
    else:
        assert 2 * batch == rows
        upper = lax.broadcasted_iota(jnp.int32, (rows, sw), 0) >= batch

        def body(t, c):
            cr, ci = c
            r0 = pl.multiple_of(t * rows, rows)
            xr, xi = hbuf[pl.ds(r0, rows), :sw], hbuf[pl.ds(r0, rows), sw:]
            ar, ai = step(pltpu.roll(cr, batch, axis=0), pltpu.roll(ci, batch, axis=0), xr, xi)
            br, bi = step(pltpu.roll(ar, batch, axis=0), pltpu.roll(ai, batch, axis=0), xr, xi)
            hr, hi = jnp.where(upper, br, ar), jnp.where(upper, bi, ai)
            hbuf[pl.ds(r0, rows), :sw] = hr
            hbuf[pl.ds(r0, rows), sw:] = hi
            return hr, hi

    cr, ci = lax.fori_loop(0, rc // rows, body, (carry[:, :sw], carry[:, sw:]))
    carry[:, :sw] = cr
    carry[:, sw:] = ci
    hfin_ref[:, :sw] = cr
    hfin_ref[:, sw:] = ci

    n_out = cre_ref.shape[0]
    kout, wout = cre_ref.shape[1], cre_ref.shape[2]
    ys = []
    for o in range(n_out):
        hr = hbuf[:, o * kout:(o + 1) * kout].astype(BF16)
        hi = hbuf[:, sw + o * kout:sw + (o + 1) * kout].astype(BF16)
        ys.append(_dot(hr, cre_ref[o]) + _dot(hi, cim_ref[o]))
    y = jnp.concatenate(ys, axis=1) + d_ref[...] * u
    y_ref[...] = jax.nn.gelu(y).astype(BF16)
    del wout


def _ssm(u_tb, h0, lre, lim, bre, bim, cre, cim, dskip, *, batch):
    n, ch = u_tb.shape
    sw = lre.shape[1]
    rows = h0.shape[0]
    rc = min(SSM_CHUNK_ROWS, n)
    assert n % rc == 0 and rc % rows == 0
    full = lambda a: pl.BlockSpec(a.shape, lambda i: (0,) * a.ndim)
    return pl.pallas_call(
        functools.partial(_ssm_kernel, batch=batch, state_w=sw),
        grid=(n // rc,),
        in_specs=[pl.BlockSpec((rc, ch), lambda i: (i, 0)), full(h0), full(lre), full(lim),
                  full(bre), full(bim), full(cre), full(cim), full(dskip)],
        out_specs=(pl.BlockSpec((rc, ch), lambda i: (i, 0)), full(h0)),
        out_shape=(jax.ShapeDtypeStruct((n, ch), BF16), jax.ShapeDtypeStruct(h0.shape, F32)),
        scratch_shapes=[pltpu.VMEM((rc, 2 * sw), F32), pltpu.VMEM(h0.shape, F32)],
        compiler_params=_cparams("arbitrary"),
        name="ssm",
    )(u_tb, h0, lre, lim, bre, bim, cre, cim, dskip)


def _merge_kernel(x_ref, o_ref, ys_ref, cbz_ref, g_ref, wao_ref, wga_ref, wgb_ref, wco_ref, wo_ref, out_ref):
    d = x_ref.shape[1]
    ys = ys_ref[...]
    attn_out = _dot(o_ref[...], wao_ref[...])
    ssm_out = _dot(ys, wga_ref[...]) * jax.nn.sigmoid(_dot(ys, wgb_ref[...]))
    conv_out = _dot(cbz_ref[...], wco_ref[...])
    merged = (g_ref[:, 0:d] * attn_out + g_ref[:, d:2 * d] * ssm_out + g_ref[:, 2 * d:3 * d] * conv_out)
    out_ref[...] = x_ref[...] + _dot(merged.astype(BF16), wo_ref[...])


def _merge(x, o, ys, cbz, g, wao, wga, wgb, wco, wo):
    n, d = x.shape
    tm = min(ROW_TILE, n)
    row = lambda a: pl.BlockSpec((tm, a.shape[1]), lambda i: (i, 0))
    return pl.pallas_call(
        _merge_kernel,
        grid=(n // tm,),
        in_specs=[row(x), row(o), row(ys), row(cbz), row(g)] + [_resident()] * 5,
        out_specs=row(x),
        out_shape=jax.ShapeDtypeStruct((n, d), F32),
        compiler_params=_cparams("parallel"),
        name="merge",
    )(x, o, ys, cbz, g, wao, wga, wgb, wco, wo)


def _ffn_kernel(x_ref, pe_ref, gffn_ref, wg_ref, wu_ref, wd_ref, gple_ref, wpg_ref, wpp_ref, out_ref):
    x = x_ref[...]
    h = _rms(x, gffn_ref[...]).astype(BF16)
    act = jax.nn.silu(_dot(h, wg_ref[...])) * _dot(h, wu_ref[...])
    x = x + _dot(act.astype(BF16), wd_ref[...])
    h = _rms(x, gple_ref[...]).astype(BF16)
    gate = jax.nn.sigmoid(_dot(h, wpg_ref[...]))
    out_ref[...] = x + gate * _dot(pe_ref[...].astype(BF16), wpp_ref[...])


def _ffn(x, pe, gffn, wg, wu, wd, gple, wpg, wpp):
    n, d = x.shape
    tm = min(ROW_TILE, n)
    row = lambda a: pl.BlockSpec((tm, a.shape[1]), lambda i: (i, 0))
    full = lambda a: pl.BlockSpec(a.shape, lambda i: (0,) * a.ndim)
    return pl.pallas_call(
        _ffn_kernel,
        grid=(n // tm,),
        in_specs=[row(x), row(pe), full(gffn), _resident(), _resident(), _resident(), full(gple),
                  _resident(), _resident()],
        out_specs=row(x),
        out_shape=jax.ShapeDtypeStruct((n, d), F32),
        compiler_params=_cparams("parallel"),
        name="ffn",
    )(x, pe, gffn, wg, wu, wd, gple, wpg, wpp)


def _block_diag(blocks):
    n, r, c = blocks.shape
    eye = jnp.eye(n, dtype=bool)
    return jnp.where(eye[:, None, :, None], blocks[:, :, None, :], 0).reshape(n * r, n * c)


def _ssm_matrices(lw):
    g, p = lw["a_re"].shape
    lre, lim, bbre, bbim = _ssm_params(lw["a_re"], lw["a_im"], lw["log_dt"],
                                       jnp.swapaxes(lw["b_re"], 1, 2), jnp.swapaxes(lw["b_im"], 1, 2))
    c = bbre.shape[1]
    gin = 128 // c
    b_in = lambda bb: jax.vmap(_block_diag)(bb.reshape(g // gin, gin, c, p)).astype(BF16)
    gout = MXU_DIM // c
    c_out = lambda cc: jax.vmap(_block_diag)(
        jnp.swapaxes(cc, 1, 2).reshape(g // gout, gout, p, c)).astype(BF16)
    return (lre.reshape(1, g * p), lim.reshape(1, g * p), b_in(bbre), b_in(bbim),
            c_out(lw["c_re"]), c_out(-lw["c_im"]))


def _layer(x, pe, attend, h0_re, h0_im, conv_state, lw, mats, lambda_init, *, batch, seq_len):
    n, d = x.shape
    outs = _in_proj(x, lw["g_mix"], lw["w_in"], lw["qn"], lw["kn"], lw["seg"], lw["w_conv"], conv_state,
                    seq_len=seq_len)
    q, k_f, k_b, v_f, v_b, u, z, cbz, gates = outs
    o = attend(q, k_b, v_b)

    lre, lim, bre, bim, cre, cim = mats
    sw = lre.shape[1]
    ch = u.shape[1]
    u_tb = u.reshape(batch, seq_len, ch).swapaxes(0, 1).reshape(n, ch)
    h0 = jnp.concatenate([h0_re.reshape(batch, sw), h0_im.reshape(batch, sw)], axis=1)
    rows = max(batch, SUBLANES)
    if batch < rows:
        h0 = jnp.concatenate([jnp.zeros((rows - batch, 2 * sw), F32), h0], axis=0)
    ys_tb, hfin = _ssm(u_tb, h0, lre, lim, bre, bim, cre, cim, lw["d_skip"], batch=batch)
    ys = ys_tb.reshape(seq_len, batch, ch).swapaxes(0, 1).reshape(n, ch)
    hfin = hfin[rows - batch:]
    groups = sw // SSM_STATE

    x1 = _merge(x, o, ys, cbz, gates, lw["w_attn_out"], lw["w_glu_a"], lw["w_glu_b"], lw["w_conv_out"], lw["w_o"])
    x2 = _ffn(x1, pe, lw["g_ffn"], lw["w_ffn_gate"], lw["w_ffn_up"], lw["w_ffn_down"], lw["g_ple"],
              lw["w_ple_gate"], lw["w_ple_proj"])
    return (x2,
            k_f.reshape(batch, seq_len, N_HEADS, V_DIM),
            v_f.reshape(batch, seq_len, N_HEADS, V_DIM),
            hfin[:, :sw].reshape(batch, groups, SSM_STATE),
            hfin[:, sw:].reshape(batch, groups, SSM_STATE),
            z.reshape(batch, seq_len, ch)[:, seq_len - (CONV_W - 1):],
            k_b, v_b)


def kernel(x_prompt, x_sample, cache_k, cache_v, state_ssm_re, state_ssm_im, state_conv, page_table, p_prompt, p_sample, rel_table, g_mix, w_in, q_norm, k_norm, lambda_q1, lambda_k1, lambda_q2, lambda_k2, sub_norm, w_attn_out, ssm_a_re, ssm_a_im, ssm_log_dt, ssm_b_re, ssm_b_im, ssm_c_re, ssm_c_im, ssm_d, w_glu_a, w_glu_b, w_conv, w_conv_out, w_o, g_ffn, w_ffn_gate, w_ffn_up, w_ffn_down, g_ple, w_ple_gate, w_ple_proj):
    bp, seq, d = x_prompt.shape
    db, t_new, _ = x_sample.shape
    depth = w_in.shape[0]
    page = cache_k.shape[2]
    past = page_table.shape[1] * page
    conv_dim = w_conv.shape[2]

    bias_p = _prompt_bias(rel_table, ATTN_TILE)
    bias_s, bias_new = _sample_bias(rel_table, past, t_new, page)
    seg = (_block_diag(jnp.ones((MXU_DIM // HEAD_DIM, HEAD_DIM, HEAD_DIM), F32)) / HEAD_DIM).astype(BF16)
    reps = d // HEAD_DIM

    xp = x_prompt.reshape(bp * seq, d)
    xs = x_sample.reshape(db * t_new, d)
    outs_p, outs_s = [], []
    for i in range(depth):
        lambda_init = 0.8 - 0.6 * math.exp(-0.3 * i)
        bf = lambda w: w[i].astype(BF16)
        vec = lambda w: w[i].reshape(1, -1)
        lw = dict(
            g_mix=vec(g_mix), w_in=bf(w_in), w_conv=w_conv[i], seg=seg,
            qn=jnp.tile(q_norm[i], reps).reshape(1, d) * (HEAD_DIM ** -0.5 * LOG2E),
            kn=jnp.tile(k_norm[i], reps).reshape(1, d),
            w_attn_out=bf(w_attn_out), w_glu_a=bf(w_glu_a), w_glu_b=bf(w_glu_b),
            w_conv_out=bf(w_conv_out), w_o=bf(w_o), g_ffn=vec(g_ffn), w_ffn_gate=bf(w_ffn_gate),
            w_ffn_up=bf(w_ffn_up), w_ffn_down=bf(w_ffn_down), g_ple=vec(g_ple),
            w_ple_gate=bf(w_ple_gate), w_ple_proj=bf(w_ple_proj),
            a_re=ssm_a_re[i], a_im=ssm_a_im[i], log_dt=ssm_log_dt[i], b_re=ssm_b_re[i], b_im=ssm_b_im[i],
            c_re=ssm_c_re[i], c_im=ssm_c_im[i], d_skip=vec(ssm_d),
        )
        mats = _ssm_matrices(lw)
        lam_args = tuple(w[i].reshape(1, HEAD_DIM) for w in (lambda_q1, lambda_k1, lambda_q2, lambda_k2))
        sn_scaled = sub_norm[i] * (1.0 - lambda_init)

        def attend_prompt(q, k_b, v_b):
            vt = v_b.reshape(bp, seq, N_HEADS, V_DIM).transpose(0, 2, 3, 1)
            sn = jnp.broadcast_to(sn_scaled[:, None], (V_DIM, ATTN_TILE))
            return _prompt_attention(q, k_b, vt, bias_p, sn, *lam_args, batch=bp, seq_len=seq,
                                     lambda_init=lambda_init)

        def attend_sample(q, k_b, v_b):
            qh = q.astype(F32).reshape(db, t_new, N_HEADS, V_DIM).transpose(0, 2, 1, 3)
            q_rows = jnp.broadcast_to(qh[:, :, None], (db, N_HEADS, 2, t_new, V_DIM)).reshape(db, -1, V_DIM)
            pad = lambda a: jnp.pad(a.astype(F32).reshape(db, t_new, d), ((0, 0), (0, SUBLANES - t_new), (0, 0)))
            o = _sample_attention(q_rows, cache_k, cache_v, page_table, bias_s, pad(k_b), pad(v_b),
                                  bias_new, sn_scaled.reshape(1, V_DIM), *lam_args, layer=i, t_new=t_new,
                                  lambda_init=lambda_init)
            o = o.reshape(db, N_HEADS, 2, t_new, V_DIM)[:, :, 0]
            return o.transpose(0, 2, 1, 3).reshape(db * t_new, d)

        zs = jnp.zeros((bp, d // 2 // SSM_GROUP * SSM_STATE), F32)
        rp = _layer(xp, p_prompt[i].reshape(bp * seq, -1), attend_prompt, zs, zs,
                    jnp.zeros((bp, CONV_W - 1, conv_dim), F32), lw, mats, lambda_init, batch=bp, seq_len=seq)
        rs = _layer(xs, p_sample[i].reshape(db * t_new, -1), attend_sample, state_ssm_re[i], state_ssm_im[i],
                    state_conv[i], lw, mats, lambda_init, batch=db, seq_len=t_new)
        xp, xs = rp[0], rs[0]
        outs_p.append(rp[1:6])
        outs_s.append(rs[1:6])

    stack = lambda outs, j: jnp.stack([o[j] for o in outs])
    return ((xp.reshape(bp, seq, d), xs.reshape(db, t_new, d))
            + tuple(stack(outs_p, j) for j in range(5))
            + tuple(stack(outs_s, j) for j in range(5)))
```

```python
import functools
import math

import jax
import jax.numpy as jnp
from jax import lax
from jax.experimental import pallas as pl
from jax.experimental.pallas import tpu as pltpu

F32 = jnp.float32
BF16 = jnp.bfloat16

N_HEADS = 8
HEAD_DIM = 64
V_DIM = 2 * HEAD_DIM
SSM_GROUP = 16
SSM_STATE = 64
CONV_W = 3
N_BRANCH = 3
REL_BUCKETS = 32
REL_MAX_DIST = 128
EPS = 1e-6
NEG = -1e30
LOG2E = 1.4426950408889634

V7X_VMEM_BYTES = 64 * 1024 * 1024
VMEM_LIMIT_BYTES = (V7X_VMEM_BYTES * 7) // 8
MXU_DIM = 256
SUBLANES = 8
ROW_TILE = 256
ATTN_TILE = 256
ATTN_HEADS_PER_STEP = 4
ONES_ROWS = 16
BIAS_FAR = 2
BIAS_MASKED = 3
SSM_CHUNK_ROWS = 256
PAGES_PER_STEP = 16
PAGE_GROUP = 2


def _cparams(*sem):
    return pltpu.CompilerParams(dimension_semantics=sem, vmem_limit_bytes=VMEM_LIMIT_BYTES)


def _resident():
    return pl.BlockSpec(memory_space=pltpu.VMEM)


def _rms(x, g):
    ms = jnp.mean(x * x, axis=-1, keepdims=True)
    return x * lax.rsqrt(ms + EPS) * g


def _dot(a, b):
    return jnp.dot(a, b, preferred_element_type=F32)


def _head_norm(acc, seg, gain):
    outs = []
    for c in range(acc.shape[1] // MXU_DIM):
        a = acc[:, c * MXU_DIM:(c + 1) * MXU_DIM]
        ms = _dot((a * a).astype(BF16), seg)
        outs.append(a * lax.rsqrt(ms + EPS))
    return jnp.concatenate(outs, axis=1) * gain


def _in_proj_kernel(*refs, tm, seq_len, d_model, conv_dim, carry_mode):
    if carry_mode:
        (x_ref, gmix_ref, w_ref, qn_ref, kn_ref, seg_ref, wconv_ref, buf_ref,
         q_ref, kf_ref, kb_ref, vf_ref, vb_ref, u_ref, z_ref, cbz_ref, g_ref, carry_ref) = refs
    else:
        (x_ref, gmix_ref, w_ref, qn_ref, kn_ref, seg_ref, wconv_ref, p1_ref, p2_ref,
         q_ref, kf_ref, kb_ref, vf_ref, vb_ref, u_ref, z_ref, cbz_ref, g_ref) = refs
    d = d_model
    h = _rms(x_ref[...], gmix_ref[...]).astype(BF16)

    def proj(c0, c1):
        return _dot(h, w_ref[:, c0:c1])

    seg = seg_ref[...]
    q_ref[...] = _head_norm(proj(0, d), seg, qn_ref[...]).astype(BF16)
    k = _head_norm(proj(d, 2 * d), seg, kn_ref[...])
    kf_ref[...] = k
    kb_ref[...] = k.astype(BF16)
    v = proj(2 * d, 3 * d)
    vf_ref[...] = v
    vb_ref[...] = v.astype(BF16)
    c0 = 3 * d
    ucb = proj(c0, c0 + 2 * conv_dim)
    u_ref[...] = ucb[:, :conv_dim]
    cb = ucb[:, conv_dim:]
    ccx = proj(c0 + 2 * conv_dim, c0 + 4 * conv_dim)
    z = ccx[:, :conv_dim] * ccx[:, conv_dim:]
    z_ref[...] = z

    row = lax.broadcasted_iota(jnp.int32, z.shape, 0)
    zr1 = pltpu.roll(z, 1, axis=0)
    zr2 = pltpu.roll(z, 2, axis=0)
    if carry_mode:
        tiles_per_seq = seq_len // tm

        @pl.when(pl.program_id(0) % tiles_per_seq == 0)
        def _():
            carry_ref[...] = buf_ref[...]

        c_m2 = carry_ref[SUBLANES - 2:SUBLANES - 1, :]
        c_m1 = carry_ref[SUBLANES - 1:SUBLANES, :]
        z1 = jnp.where(row == 0, c_m1, zr1)
        z2 = jnp.where(row == 0, c_m2, jnp.where(row == 1, c_m1, zr2))
        carry_ref[...] = z[tm - SUBLANES:tm, :]
    else:
        t = lax.rem(row, seq_len)
        z1 = jnp.where(t >= 1, zr1, p1_ref[...])
        z2 = jnp.where(t >= 2, zr2, p2_ref[...])
    zc = wconv_ref[0:1, :] * z2 + wconv_ref[1:2, :] * z1 + wconv_ref[2:3, :] * z
    cbz_ref[...] = (cb * zc).astype(BF16)

    g0 = c0 + 4 * conv_dim
    for c in range(N_BRANCH):
        g_ref[:, c * d:(c + 1) * d] = jax.nn.sigmoid(proj(g0 + c * d, g0 + (c + 1) * d))


def _in_proj(x, gmix, w_in, qn, kn, seg, wconv, conv_state, *, seq_len):
    n, d = x.shape
    conv_dim = wconv.shape[1]
    tm = min(ROW_TILE, n)
    carry_mode = seq_len >= tm
    assert n % tm == 0 and seq_len >= CONV_W - 1
    assert (seq_len % tm == 0) if carry_mode else (tm % seq_len == 0)
    n_seq = n // seq_len
    row = lambda w: pl.BlockSpec((tm, w), lambda i: (i, 0))
    full = lambda a: pl.BlockSpec(a.shape, lambda i: (0,) * a.ndim)
    if carry_mode:
        buf = jnp.zeros((n_seq, SUBLANES, conv_dim), F32).at[:, SUBLANES - 2:, :].set(conv_state)
        tiles_per_seq = seq_len // tm
        state_args = (buf,)
        state_specs = [pl.BlockSpec((None, SUBLANES, conv_dim), lambda i: (i // tiles_per_seq, 0, 0))]
        scratch = [pltpu.VMEM((SUBLANES, conv_dim), F32)]
    else:
        zeros = jnp.zeros((n_seq, seq_len, conv_dim), F32)
        p1 = zeros.at[:, 0].set(conv_state[:, 1]).reshape(n, conv_dim)
        p2 = zeros.at[:, 0].set(conv_state[:, 0]).at[:, 1].set(conv_state[:, 1]).reshape(n, conv_dim)
        state_args = (p1, p2)
        state_specs = [row(conv_dim), row(conv_dim)]
        scratch = []
    out_shape = (
        jax.ShapeDtypeStruct((n, d), BF16),
        jax.ShapeDtypeStruct((n, d), F32),
        jax.ShapeDtypeStruct((n, d), BF16),
        jax.ShapeDtypeStruct((n, d), F32),
        jax.ShapeDtypeStruct((n, d), BF16),
        jax.ShapeDtypeStruct((n, conv_dim), F32),
        jax.ShapeDtypeStruct((n, conv_dim), F32),
        jax.ShapeDtypeStruct((n, conv_dim), BF16),
        jax.ShapeDtypeStruct((n, N_BRANCH * d), F32),
    )
    out_specs = (row(d), row(d), row(d), row(d), row(d), row(conv_dim), row(conv_dim),
                 row(conv_dim), row(N_BRANCH * d))
    return pl.pallas_call(
        functools.partial(_in_proj_kernel, tm=tm, seq_len=seq_len, d_model=d, conv_dim=conv_dim,
                          carry_mode=carry_mode),
        grid=(n // tm,),
        in_specs=[row(d), full(gmix), _resident(), full(qn), full(kn), full(seg), full(wconv)] + state_specs,
        out_specs=out_specs,
        out_shape=out_shape,
        scratch_shapes=scratch,
        compiler_params=_cparams("arbitrary"),
        name="in_proj",
    )(x, gmix, w_in, qn, kn, seg, wconv, *state_args)


def _rel_bias(n, tbl_ref, head):
    max_exact = REL_BUCKETS // 2
    nc = jnp.maximum(n, 0)
    nf = jnp.maximum(nc, max_exact).astype(F32)
    large = max_exact + (jnp.log(nf / max_exact) / math.log(REL_MAX_DIST / max_exact)
                         * (REL_BUCKETS - max_exact)).astype(jnp.int32)
    large = jnp.minimum(large, REL_BUCKETS - 1)
    bucket = jnp.where(nc < max_exact, nc, large)
    val = jnp.zeros(n.shape, F32)
    for b in range(REL_BUCKETS):
        val = jnp.where(bucket == b, tbl_ref[b, head], val)
    return jnp.where(n >= 0, val * LOG2E, NEG)


def _prompt_bias_kernel(tbl_ref, o_ref, *, tile):
    head, d = pl.program_id(0), pl.program_id(1)
    key = lax.broadcasted_iota(jnp.int32, (tile, tile), 0)
    qry = lax.broadcasted_iota(jnp.int32, (tile, tile), 1)
    n = jnp.where(d == BIAS_MASKED, -1, d * tile + qry - key)
    b = _rel_bias(n, tbl_ref, head)
    o_ref[...] = jnp.concatenate([b, b], axis=1)


def _prompt_bias(rel_table, tile):
    assert tile >= REL_MAX_DIST
    return pl.pallas_call(
        functools.partial(_prompt_bias_kernel, tile=tile),
        grid=(N_HEADS, BIAS_MASKED + 1),
        in_specs=[pl.BlockSpec(memory_space=pltpu.SMEM)],
        out_specs=pl.BlockSpec((None, None, tile, 2 * tile), lambda h, d: (h, d, 0, 0)),
        out_shape=jax.ShapeDtypeStruct((N_HEADS, BIAS_MASKED + 1, tile, 2 * tile), F32),
        compiler_params=_cparams("arbitrary", "arbitrary"),
        name="prompt_bias",
    )(rel_table)


def _sample_bias_kernel(tbl_ref, o_ref, onew_ref, *, past, t_new, page):
    shape = (N_HEADS * 2 * t_new, page * N_HEADS)
    r = lax.broadcasted_iota(jnp.int32, shape, 0)
    col = lax.broadcasted_iota(jnp.int32, shape, 1)
    first_key = jnp.where(pl.program_id(0) == 0, 0, past - page)
    n = past + lax.rem(r, t_new) - (first_key + col // N_HEADS)
    rn = lax.broadcasted_iota(jnp.int32, onew_ref.shape, 0)
    jn = lax.broadcasted_iota(jnp.int32, onew_ref.shape, 1)
    nn = jnp.where(jn < t_new, lax.rem(rn, t_new) - jn, -1)
    acc = jnp.full(shape, NEG, F32)
    accn = jnp.zeros(onew_ref.shape, F32)
    rows_per_head = 2 * t_new
    for head in range(N_HEADS):
        own = (r // rows_per_head == head) & (lax.rem(col, N_HEADS) == head)
        acc = jnp.where(own, _rel_bias(n, tbl_ref, head), acc)
        accn = jnp.where(rn // rows_per_head == head, _rel_bias(nn, tbl_ref, head), accn)
    o_ref[...] = acc
    onew_ref[...] = accn


def _sample_bias(rel_table, past, t_new, page):
    assert page >= REL_MAX_DIST and past >= 2 * page
    rows = N_HEADS * 2 * t_new
    return pl.pallas_call(
        functools.partial(_sample_bias_kernel, past=past, t_new=t_new, page=page),
        grid=(2,),
        in_specs=[pl.BlockSpec(memory_space=pltpu.SMEM)],
        out_specs=(pl.BlockSpec((None, rows, page * N_HEADS), lambda s: (s, 0, 0)),
                   pl.BlockSpec((rows, SUBLANES), lambda s: (0, 0))),
        out_shape=(jax.ShapeDtypeStruct((2, rows, page * N_HEADS), F32),
                   jax.ShapeDtypeStruct((rows, SUBLANES), F32)),
        compiler_params=_cparams("arbitrary"),
        name="sample_bias",
    )(rel_table)


def _lambda(lq1, lk1, lq2, lk2, lambda_init):
    s1 = jnp.sum(lq1[...] * lk1[...], axis=-1, keepdims=True)
    s2 = jnp.sum(lq2[...] * lk2[...], axis=-1, keepdims=True)
    return jnp.exp(s1) - jnp.exp(s2) + lambda_init


def _flash_kernel(q_ref, k_ref, vt_ref, bias_ref, sn_ref, lq1, lk1, lq2, lk2, o_ref, m_ref, acc_ref, s_ref,
                  *, tile, lambda_init):
    i = pl.program_id(2)
    heads = m_ref.shape[0]
    lane = lax.broadcasted_iota(jnp.int32, (tile, V_DIM), 1)
    zero = jnp.zeros((tile, V_DIM), BF16)
    q2 = []
    for hh in range(heads):
        q = q_ref[:, hh * V_DIM:(hh + 1) * V_DIM]
        q2.append(jnp.concatenate([jnp.where(lane < HEAD_DIM, q, zero), jnp.where(lane >= HEAD_DIM, q, zero)],
                                  axis=0))
    m_ref[...] = jnp.full(m_ref.shape, NEG, F32)
    acc_ref[...] = jnp.zeros(acc_ref.shape, F32)

    def key_start(j):
        return pl.multiple_of(jnp.minimum(j, i) * tile, tile)

    def scores(hh, j):
        return lax.dot_general(k_ref[pl.ds(key_start(j), tile), hh * V_DIM:(hh + 1) * V_DIM], q2[hh],
                               (((1,), (1,)), ((), ())), preferred_element_type=F32)

    def update(hh, j, s, bias, far):
        if bias is not None:
            s = s + bias
        smax = jnp.max(s, axis=0, keepdims=True)
        if far is not None:
            smax = smax + far
        m = m_ref[hh]
        m_new = jnp.maximum(m, smax)
        alpha = jnp.exp2(m - m_new)
        p = jnp.exp2(s - (m_new if far is None else m_new - far))
        acc_ref[hh] = alpha * acc_ref[hh] + _dot(vt_ref[hh, :, pl.ds(key_start(j), tile)], p.astype(BF16))
        m_ref[hh] = m_new

    n_pairs = lax.shift_right_logical(i + 2, 1)
    n_far_pairs = lax.shift_right_logical(jnp.maximum(i - 1, 0), 1)

    def pair(jj, far):
        for cur in (0, 1):
            j = 2 * jj + cur
            kind = jnp.where(j > i, BIAS_MASKED, jnp.minimum(i - j, BIAS_FAR))
            for hh in range(heads):
                s_ref[1 - cur, hh] = scores(hh, j + 1)
                if far:
                    update(hh, j, s_ref[cur, hh], None, bias_ref[hh, BIAS_FAR, 0:1, :])
                else:
                    update(hh, j, s_ref[cur, hh], bias_ref[hh, kind], None)

    for hh in range(heads):
        s_ref[0, hh] = scores(hh, 0)

    def far_body(jj, c):
        pair(jj, True)
        return c

    def near_body(jj, c):
        pair(jj, False)
        return c

    lax.fori_loop(0, n_far_pairs, far_body, 0)
    lax.fori_loop(n_far_pairs, n_pairs, near_body, 0)
    lam = _lambda(lq1, lk1, lq2, lk2, lambda_init)
    for hh in range(heads):
        o = acc_ref[hh, :V_DIM, :] * (1.0 / acc_ref[hh, V_DIM:V_DIM + 1, :])
        od = o[:, :tile] - lam * o[:, tile:]
        ms = jnp.mean(od * od, axis=0, keepdims=True)
        on = od * lax.rsqrt(ms + EPS) * sn_ref[...]
        o_ref[:, hh * V_DIM:(hh + 1) * V_DIM] = on.T.astype(BF16)


def _prompt_attention(q, k, vt, bias, sn, lq1, lk1, lq2, lk2, *, batch, seq_len, lambda_init):
    tile = ATTN_TILE
    hp = ATTN_HEADS_PER_STEP
    nq = seq_len // tile
    va = vt.shape[2]
    assert seq_len % tile == 0 and N_HEADS % hp == 0 and va == V_DIM + ONES_ROWS
    small = pl.BlockSpec((1, HEAD_DIM), lambda b, h, i: (0, 0))
    return pl.pallas_call(
        functools.partial(_flash_kernel, tile=tile, lambda_init=lambda_init),
        grid=(batch, N_HEADS // hp, nq),
        in_specs=[
            pl.BlockSpec((tile, hp * V_DIM), lambda b, h, i: (b * nq + i, h)),
            pl.BlockSpec((seq_len, hp * V_DIM), lambda b, h, i: (b, h)),
            pl.BlockSpec((None, hp, va, seq_len), lambda b, h, i: (b, h, 0, 0)),
            pl.BlockSpec((hp, BIAS_MASKED + 1, tile, 2 * tile), lambda b, h, i: (h, 0, 0, 0)),
            pl.BlockSpec((V_DIM, tile), lambda b, h, i: (0, 0)),
            small, small, small, small,
        ],
        out_specs=pl.BlockSpec((tile, hp * V_DIM), lambda b, h, i: (b * nq + i, h)),
        out_shape=jax.ShapeDtypeStruct(q.shape, BF16),
        scratch_shapes=[pltpu.VMEM((hp, 1, 2 * tile), F32), pltpu.VMEM((hp, va, 2 * tile), F32), pltpu.VMEM((2, hp, tile, 2 * tile), F32)],
        compiler_params=_cparams("arbitrary", "arbitrary", "arbitrary"),
        name="prompt_attention",
    )(q, k, vt, bias, sn, lq1, lk1, lq2, lk2)


def _paged_kernel(*refs, n_pages, t_new, lambda_init):
    k_refs = refs[1:1 + n_pages]
    v_refs = refs[1 + n_pages:1 + 2 * n_pages]
    (q_ref, bias_ref, kn_ref, vn_ref, bnew_ref, sn_ref, lq1, lk1, lq2, lk2,
     o_ref, m_ref, l_ref, acc_ref) = refs[1 + 2 * n_pages:]
    step = pl.program_id(1)
    rph = 2 * t_new
    rows = N_HEADS * rph
    cols = k_refs[0].shape[0]

    r = lax.broadcasted_iota(jnp.int32, (rows, V_DIM), 0)
    lane = lax.broadcasted_iota(jnp.int32, (rows, V_DIM), 1)
    own = (lane // HEAD_DIM) == lax.rem(r // t_new, 2)
    q = jnp.where(own, q_ref[...], 0.0)

    @pl.when(step == 0)
    def _():
        m_ref[...] = jnp.full(m_ref.shape, NEG, F32)
        l_ref[...] = jnp.zeros(l_ref.shape, F32)
        acc_ref[...] = jnp.zeros(acc_ref.shape, F32)

    def head_rows(x, h):
        return x[h * rph:(h + 1) * rph, :]

    def update(s, pv_fn):
        m = m_ref[...]
        m_new = jnp.maximum(m, jnp.max(s, axis=-1, keepdims=True))
        alpha = jnp.exp2(m - m_new)
        p = jnp.exp2(s - m_new)
        l_ref[...] = alpha * l_ref[...] + jnp.sum(p, axis=-1, keepdims=True)
        acc_ref[...] = alpha * acc_ref[...] + pv_fn(p)
        m_ref[...] = m_new

    def scores(qh, kb):
        return lax.dot_general(qh, kb, (((1,), (1,)), ((), ())), preferred_element_type=F32)

    last = step == pl.num_programs(1) - 1

    def group_scores(g):
        parts = []
        for c in range(g * PAGE_GROUP, (g + 1) * PAGE_GROUP):
            bias = bias_ref[0]
            if c == n_pages - 1:
                bias = jnp.where(last, bias_ref[1], bias)
            parts.append(scores(q, k_refs[c][...]) + bias)
        return jnp.concatenate(parts, axis=1)

    def group_pv(g):
        def pv(p):
            o = None
            for n, c in enumerate(range(g * PAGE_GROUP, (g + 1) * PAGE_GROUP)):
                t = _dot(p[:, n * cols:(n + 1) * cols], v_refs[c][...])
                o = t if o is None else o + t
            return o
        return pv

    n_groups = n_pages // PAGE_GROUP
    s = group_scores(0)
    for g in range(n_groups):
        s_next = group_scores(g + 1) if g + 1 < n_groups else None
        update(s, group_pv(g))
        s = s_next

    @pl.when(step == pl.num_programs(1) - 1)
    def _():
        lanes = lambda ref, h: ref[:, h * V_DIM:(h + 1) * V_DIM]
        s_new = jnp.concatenate([scores(head_rows(q, h), lanes(kn_ref, h)) for h in range(N_HEADS)], axis=0)
        update(s_new + bnew_ref[...],
               lambda p: jnp.concatenate([_dot(head_rows(p, h), lanes(vn_ref, h)) for h in range(N_HEADS)], axis=0))
        a = acc_ref[...] * (1.0 / l_ref[...])
        a2 = pltpu.roll(a, rows - t_new, axis=0)
        dm = a - _lambda(lq1, lk1, lq2, lk2, lambda_init) * a2
        ms = jnp.mean(dm * dm, axis=-1, keepdims=True)
        o_ref[...] = (dm * lax.rsqrt(ms + EPS) * sn_ref[...]).astype(BF16)


def _sample_attention(q_rows, cache_k, cache_v, page_table, bias, k_new, v_new, bias_new, sn,
                      lq1, lk1, lq2, lk2, *, layer, t_new, lambda_init):
    db, n_tbl = page_table.shape
    depth, n_pool, page, n_heads, v_dim = cache_k.shape
    d = n_heads * v_dim
    rows = N_HEADS * 2 * t_new
    npg = PAGES_PER_STEP
    assert n_tbl % npg == 0 and npg % PAGE_GROUP == 0 and 2 * t_new == SUBLANES
    assert n_heads == N_HEADS == SUBLANES and v_dim == V_DIM
    cache_k = cache_k.reshape(depth, n_pool, page * n_heads, v_dim)
    cache_v = cache_v.reshape(depth, n_pool, page * n_heads, v_dim)

    def page_spec(c):
        return pl.BlockSpec((None, None, page * n_heads, v_dim),
                            lambda b, s, pt: (layer, pt[b, s * npg + c], 0, 0))

    small = pl.BlockSpec((1, HEAD_DIM), lambda b, s, pt: (0, 0))
    in_specs = [page_spec(c) for c in range(npg)] * 2 + [
        pl.BlockSpec((None, rows, v_dim), lambda b, s, pt: (b, 0, 0)),
        pl.BlockSpec(bias.shape, lambda b, s, pt: (0, 0, 0)),
        pl.BlockSpec((None, SUBLANES, d), lambda b, s, pt: (b, 0, 0)),
        pl.BlockSpec((None, SUBLANES, d), lambda b, s, pt: (b, 0, 0)),
        pl.BlockSpec((rows, SUBLANES), lambda b, s, pt: (0, 0)),
        pl.BlockSpec((1, v_dim), lambda b, s, pt: (0, 0)),
        small, small, small, small,
    ]
    grid_spec = pltpu.PrefetchScalarGridSpec(
        num_scalar_prefetch=1,
        grid=(db, n_tbl // npg),
        in_specs=in_specs,
        out_specs=pl.BlockSpec((None, rows, v_dim), lambda b, s, pt: (b, 0, 0)),
        scratch_shapes=[pltpu.VMEM((rows, 1), F32), pltpu.VMEM((rows, 1), F32), pltpu.VMEM((rows, v_dim), F32)],
    )
    return pl.pallas_call(
        functools.partial(_paged_kernel, n_pages=npg, t_new=t_new, lambda_init=lambda_init),
        grid_spec=grid_spec,
        out_shape=jax.ShapeDtypeStruct((db, rows, v_dim), BF16),
        compiler_params=_cparams("arbitrary", "arbitrary"),
        name="sample_attention",
    )(page_table, *([cache_k] * npg), *([cache_v] * npg), q_rows, bias, k_new, v_new, bias_new, sn,
      lq1, lk1, lq2, lk2)


def _ssm_param_kernel(are_ref, aim_ref, ldt_ref, bre_ref, bim_ref, lre_ref, lim_ref, bbre_ref, bbim_ref,
                      lbre_ref, lbim_ref):
    ar, ai = are_ref[...], aim_ref[...]
    dt = jnp.exp(ldt_ref[...])
    mag = jnp.exp(ar * dt)
    lr = mag * jnp.cos(ai * dt)
    li = mag * jnp.sin(ai * dt)
    lre_ref[...] = lr
    lim_ref[...] = li
    nr, ni = lr - 1.0, li
    den = ar * ar + ai * ai
    fr = ((nr * ar + ni * ai) / den)[:, None, :]
    fi = ((ni * ar - nr * ai) / den)[:, None, :]
    br, bi = bre_ref[...], bim_ref[...]
    bbr = fr * br - fi * bi
    bbi = fr * bi + fi * br
    bbre_ref[...] = bbr
    bbim_ref[...] = bbi
    lr3, li3 = lr[:, None, :], li[:, None, :]
    lbre_ref[...] = lr3 * bbr - li3 * bbi
    lbim_ref[...] = lr3 * bbi + li3 * bbr


def _ssm_params(a_re, a_im, log_dt, b_re_t, b_im_t):
    g, p = a_re.shape
    c = b_re_t.shape[1]
    mat = jax.ShapeDtypeStruct((g, c, p), F32)
    return pl.pallas_call(
        _ssm_param_kernel,
        out_shape=(jax.ShapeDtypeStruct((g, p), F32), jax.ShapeDtypeStruct((g, p), F32), mat, mat, mat, mat),
        name="ssm_params",
    )(a_re, a_im, log_dt.reshape(g, 1), b_re_t, b_im_t)


def _ssm_kernel(u_ref, h0_ref, lre_ref, lim_ref, bre_ref, bim_ref, cre_ref, cim_ref, d_ref,
                y_ref, hfin_ref, hbuf, carry, *, batch, state_w):
    rc = u_ref.shape[0]
    sw = state_w

    @pl.when(pl.program_id(0) == 0)
    def _():
        carry[...] = h0_ref[...]

    rows = carry.shape[0]
    fold = batch != rows
    u = u_ref[...]
    ub = u.astype(BF16)
    n_in = bre_ref.shape[0]
    kin, nout = u.shape[1] // n_in, bre_ref.shape[2]
    if fold:
        assert 2 * batch == rows and bre_ref.shape[1] == 2 * kin
        upper_u = lax.rem(lax.broadcasted_iota(jnp.int32, u.shape, 0), rows) >= batch
        up = jnp.where(upper_u, pltpu.roll(u, batch, axis=0), 0.0).astype(BF16)
    for j in range(n_in):
        uj = ub[:, j * kin:(j + 1) * kin]
        if fold:
            uj = jnp.concatenate([uj, up[:, j * kin:(j + 1) * kin]], axis=1)
        hbuf[:, j * nout:(j + 1) * nout] = _dot(uj, bre_ref[j])
        hbuf[:, sw + j * nout:sw + (j + 1) * nout] = _dot(uj, bim_ref[j])

    lr = jnp.broadcast_to(lre_ref[...], (rows, sw))
    li = jnp.broadcast_to(lim_ref[...], (rows, sw))
    if fold:
        upper = lax.broadcasted_iota(jnp.int32, (rows, sw), 0) >= batch
        lr, li = jnp.where(upper, lr * lr - li * li, lr), jnp.where(upper, 2.0 * lr * li, li)

    def body(t, c):
        cr, ci = c
        r0 = pl.multiple_of(t * rows, rows)
        hr = lr * cr - li * ci + hbuf[pl.ds(r0, rows), :sw]
        hi = lr * ci + li * cr + hbuf[pl.ds(r0, rows), sw:]
        hbuf[pl.ds(r0, rows), :sw] = hr
        hbuf[pl.ds(r0, rows), sw:] = hi
        if fold:
            hr = jnp.where(upper, hr, pltpu.roll(hr, batch, axis=0))
            hi = jnp.where(upper, hi, pltpu.roll(hi, batch, axis=0))
        return hr, hi

    cr, ci = lax.fori_loop(0, rc // rows, body, (carry[:, :sw], carry[:, sw:]))
    carry[:, :sw] = cr
    carry[:, sw:] = ci
    hfin_ref[:, :sw] = cr
    hfin_ref[:, sw:] = ci

    n_out = cre_ref.shape[0]
    kout, wout = cre_ref.shape[1], cre_ref.shape[2]
    ys = []
    for o in range(n_out):
        hr = hbuf[:, o * kout:(o + 1) * kout].astype(BF16)
        hi = hbuf[:, sw + o * kout:sw + (o + 1) * kout].astype(BF16)
        ys.append(_dot(hr, cre_ref[o]) + _dot(hi, cim_ref[o]))
    y = jnp.concatenate(ys, axis=1) + d_ref[...] * u
    y_ref[...] = jax.nn.gelu(y).astype(BF16)
    del wout


def _ssm(u_tb, h0, lre, lim, bre, bim, cre, cim, dskip, *, batch):
    n, ch = u_tb.shape
    sw = lre.shape[1]
    rows = h0.shape[0]
    rc = min(SSM_CHUNK_ROWS, n)
    assert n % rc == 0 and rc % rows == 0
    full = lambda a: pl.BlockSpec(a.shape, lambda i: (0,) * a.ndim)
    return pl.pallas_call(
        functools.partial(_ssm_kernel, batch=batch, state_w=sw),
        grid=(n // rc,),
        in_specs=[pl.BlockSpec((rc, ch), lambda i: (i, 0)), full(h0), full(lre), full(lim),
                  full(bre), full(bim), full(cre), full(cim), full(dskip)],
        out_specs=(pl.BlockSpec((rc, ch), lambda i: (i, 0)), full(h0)),
        out_shape=(jax.ShapeDtypeStruct((n, ch), BF16), jax.ShapeDtypeStruct(h0.shape, F32)),
        scratch_shapes=[pltpu.VMEM((rc, 2 * sw), F32), pltpu.VMEM(h0.shape, F32)],
        compiler_params=_cparams("arbitrary"),
        name="ssm",
    )(u_tb, h0, lre, lim, bre, bim, cre, cim, dskip)


def _post_kernel(x_ref, o_ref, ys_ref, cbz_ref, g_ref, pe_ref, gffn_ref, gple_ref,
                 wao_ref, wga_ref, wgb_ref, wco_ref, wo_ref, wg_ref, wu_ref, wd_ref, wpg_ref, wpp_ref, out_ref):
    d = x_ref.shape[1]
    ys = ys_ref[...]
    attn_out = _dot(o_ref[...], wao_ref[...])
    ssm_out = _dot(ys, wga_ref[...]) * jax.nn.sigmoid(_dot(ys, wgb_ref[...]))
    conv_out = _dot(cbz_ref[...], wco_ref[...])
    merged = (g_ref[:, 0:d] * attn_out + g_ref[:, d:2 * d] * ssm_out + g_ref[:, 2 * d:3 * d] * conv_out)
    x = x_ref[...] + _dot(merged.astype(BF16), wo_ref[...])
    h = _rms(x, gffn_ref[...]).astype(BF16)
    act = jax.nn.silu(_dot(h, wg_ref[...])) * _dot(h, wu_ref[...])
    x = x + _dot(act.astype(BF16), wd_ref[...])
    h = _rms(x, gple_ref[...]).astype(BF16)
    gate = jax.nn.sigmoid(_dot(h, wpg_ref[...]))
    out_ref[...] = x + gate * _dot(pe_ref[...].astype(BF16), wpp_ref[...])


def _post(x, o, ys, cbz, g, pe, gffn, gple, weights):
    n, d = x.shape
    tm = min(ROW_TILE, n)
    row = lambda a: pl.BlockSpec((tm, a.shape[1]), lambda i: (i, 0))
    full = lambda a: pl.BlockSpec(a.shape, lambda i: (0,) * a.ndim)
    return pl.pallas_call(
        _post_kernel,
        grid=(n // tm,),
        in_specs=[row(x), row(o), row(ys), row(cbz), row(g), row(pe), full(gffn), full(gple)]
        + [_resident()] * len(weights),
        out_specs=row(x),
        out_shape=jax.ShapeDtypeStruct((n, d), F32),
        compiler_params=_cparams("parallel"),
        name="post",
    )(x, o, ys, cbz, g, pe, gffn, gple, *weights)


def _block_diag(blocks):
    n, r, c = blocks.shape
    eye = jnp.eye(n, dtype=bool)
    return jnp.where(eye[:, None, :, None], blocks[:, :, None, :], 0).reshape(n * r, n * c)


def _ssm_matrices(lw):
    g, p = lw["a_re"].shape
    lre, lim, bbre, bbim, lbre, lbim = _ssm_params(
        lw["a_re"], lw["a_im"], lw["log_dt"], jnp.swapaxes(lw["b_re"], 1, 2), jnp.swapaxes(lw["b_im"], 1, 2))
    c = bbre.shape[1]
    gin = 128 // c
    b_in = lambda bb: jax.vmap(_block_diag)(bb.reshape(g // gin, gin, c, p)).astype(BF16)
    b_fold = lambda bb, lb: jnp.concatenate([b_in(bb), b_in(lb)], axis=1)
    gout = MXU_DIM // c
    c_out = lambda cc: jax.vmap(_block_diag)(
        jnp.swapaxes(cc, 1, 2).reshape(g // gout, gout, p, c)).astype(BF16)
    return dict(lam=(lre.reshape(1, g * p), lim.reshape(1, g * p)),
                b_plain=(b_in(bbre), b_in(bbim)), b_fold=(b_fold(bbre, lbre), b_fold(bbim, lbim)),
                c=(c_out(lw["c_re"]), c_out(-lw["c_im"])))


def _layer(x, pe, attend, h0_re, h0_im, conv_state, lw, mats, lambda_init, *, batch, seq_len):
    n, d = x.shape
    outs = _in_proj(x, lw["g_mix"], lw["w_in"], lw["qn"], lw["kn"], lw["seg"], lw["w_conv"], conv_state,
                    seq_len=seq_len)
    q, k_f, k_b, v_f, v_b, u, z, cbz, gates = outs
    o = attend(q, k_b, v_b)

    lre, lim = mats["lam"]
    cre, cim = mats["c"]
    sw = lre.shape[1]
    ch = u.shape[1]
    u_tb = u.reshape(batch, seq_len, ch).swapaxes(0, 1).reshape(n, ch)
    h0 = jnp.concatenate([h0_re.reshape(batch, sw), h0_im.reshape(batch, sw)], axis=1)
    rows = max(batch, SUBLANES)
    if batch < rows:
        assert rows == 2 * batch and seq_len % 2 == 0
        h0 = jnp.concatenate([h0, h0], axis=0)
        bre, bim = mats["b_fold"]
    else:
        bre, bim = mats["b_plain"]
    ys_tb, hfin = _ssm(u_tb, h0, lre, lim, bre, bim, cre, cim, lw["d_skip"], batch=batch)
    ys = ys_tb.reshape(seq_len, batch, ch).swapaxes(0, 1).reshape(n, ch)
    hfin = hfin[rows - batch:]
    groups = sw // SSM_STATE

    weights = tuple(lw[k] for k in ("w_attn_out", "w_glu_a", "w_glu_b", "w_conv_out", "w_o", "w_ffn_gate",
                                    "w_ffn_up", "w_ffn_down", "w_ple_gate", "w_ple_proj"))
    x2 = _post(x, o, ys, cbz, gates, pe, lw["g_ffn"], lw["g_ple"], weights)
    return (x2,
            k_f.reshape(batch, seq_len, N_HEADS, V_DIM),
            v_f.reshape(batch, seq_len, N_HEADS, V_DIM),
            hfin[:, :sw].reshape(batch, groups, SSM_STATE),
            hfin[:, sw:].reshape(batch, groups, SSM_STATE),
            z.reshape(batch, seq_len, ch)[:, seq_len - (CONV_W - 1):],
            k_b, v_b)


def kernel(x_prompt, x_sample, cache_k, cache_v, state_ssm_re, state_ssm_im, state_conv, page_table, p_prompt, p_sample, rel_table, g_mix, w_in, q_norm, k_norm, lambda_q1, lambda_k1, lambda_q2, lambda_k2, sub_norm, w_attn_out, ssm_a_re, ssm_a_im, ssm_log_dt, ssm_b_re, ssm_b_im, ssm_c_re, ssm_c_im, ssm_d, w_glu_a, w_glu_b, w_conv, w_conv_out, w_o, g_ffn, w_ffn_gate, w_ffn_up, w_ffn_down, g_ple, w_ple_gate, w_ple_proj):
    bp, seq, d = x_prompt.shape
    db, t_new, _ = x_sample.shape
    depth = w_in.shape[0]
    page = cache_k.shape[2]
    past = page_table.shape[1] * page
    conv_dim = w_conv.shape[2]

    bias_p = _prompt_bias(rel_table, ATTN_TILE)
    bias_s, bias_new = _sample_bias(rel_table, past, t_new, page)
    seg = (_block_diag(jnp.ones((MXU_DIM // HEAD_DIM, HEAD_DIM, HEAD_DIM), F32)) / HEAD_DIM).astype(BF16)
    reps = d // HEAD_DIM

    xp = x_prompt.reshape(bp * seq, d)
    xs = x_sample.reshape(db * t_new, d)
    outs_p, outs_s = [], []
    for i in range(depth):
        lambda_init = 0.8 - 0.6 * math.exp(-0.3 * i)
        bf = lambda w: w[i].astype(BF16)
        vec = lambda w: w[i].reshape(1, -1)
        lw = dict(
            g_mix=vec(g_mix), w_in=bf(w_in), w_conv=w_conv[i], seg=seg,
            qn=jnp.tile(q_norm[i], reps).reshape(1, d) * (HEAD_DIM ** -0.5 * LOG2E),
            kn=jnp.tile(k_norm[i], reps).reshape(1, d),
            w_attn_out=bf(w_attn_out), w_glu_a=bf(w_glu_a), w_glu_b=bf(w_glu_b),
            w_conv_out=bf(w_conv_out), w_o=bf(w_o), g_ffn=vec(g_ffn), w_ffn_gate=bf(w_ffn_gate),
            w_ffn_up=bf(w_ffn_up), w_ffn_down=bf(w_ffn_down), g_ple=vec(g_ple),
            w_ple_gate=bf(w_ple_gate), w_ple_proj=bf(w_ple_proj),
            a_re=ssm_a_re[i], a_im=ssm_a_im[i], log_dt=ssm_log_dt[i], b_re=ssm_b_re[i], b_im=ssm_b_im[i],
            c_re=ssm_c_re[i], c_im=ssm_c_im[i], d_skip=vec(ssm_d),
        )
        mats = _ssm_matrices(lw)
        lam_args = tuple(w[i].reshape(1, HEAD_DIM) for w in (lambda_q1, lambda_k1, lambda_q2, lambda_k2))
        sn_scaled = sub_norm[i] * (1.0 - lambda_init)

        def attend_prompt(q, k_b, v_b):
            vt = v_b.reshape(bp, seq, N_HEADS, V_DIM).transpose(0, 2, 3, 1)
            vt = jnp.concatenate([vt, jnp.ones((bp, N_HEADS, ONES_ROWS, seq), BF16)], axis=2)
            sn = jnp.broadcast_to(sn_scaled[:, None], (V_DIM, ATTN_TILE))
            return _prompt_attention(q, k_b, vt, bias_p, sn, *lam_args, batch=bp, seq_len=seq,
                                     lambda_init=lambda_init)

        def attend_sample(q, k_b, v_b):
            qh = q.astype(F32).reshape(db, t_new, N_HEADS, V_DIM).transpose(0, 2, 1, 3)
            q_rows = jnp.broadcast_to(qh[:, :, None], (db, N_HEADS, 2, t_new, V_DIM)).reshape(db, -1, V_DIM)
            pad = lambda a: jnp.pad(a.astype(F32).reshape(db, t_new, d), ((0, 0), (0, SUBLANES - t_new), (0, 0)))
            o = _sample_attention(q_rows, cache_k, cache_v, page_table, bias_s, pad(k_b), pad(v_b),
                                  bias_new, sn_scaled.reshape(1, V_DIM), *lam_args, layer=i, t_new=t_new,
                                  lambda_init=lambda_init)
            o = o.reshape(db, N_HEADS, 2, t_new, V_DIM)[:, :, 0]
            return o.transpose(0, 2, 1, 3).reshape(db * t_new, d)

        zs = jnp.zeros((bp, d // 2 // SSM_GROUP * SSM_STATE), F32)
        rp = _layer(xp, p_prompt[i].reshape(bp * seq, -1), attend_prompt, zs, zs,
                    jnp.zeros((bp, CONV_W - 1, conv_dim), F32), lw, mats, lambda_init, batch=bp, seq_len=seq)
        rs = _layer(xs, p_sample[i].reshape(db * t_new, -1), attend_sample, state_ssm_re[i], state_ssm_im[i],
                    state_conv[i], lw, mats, lambda_init, batch=db, seq_len=t_new)
        xp, xs = rp[0], rs[0]
        outs_p.append(rp[1:6])
        outs_s.append(rs[1:6])

    stack = lambda outs, j: jnp.stack([o[j] for o in outs])
    return ((xp.reshape(bp, seq, d), xs.reshape(db, t_new, d))
            + tuple(stack(outs_p, j) for j in range(5))
            + tuple(stack(outs_s, j) for j in range(5)))
```

```python
import functools
import math

import jax
import jax.numpy as jnp
from jax import lax
from jax.experimental import pallas as pl
from jax.experimental.pallas import tpu as pltpu

F32 = jnp.float32
BF16 = jnp.bfloat16

N_HEADS = 8
HEAD_DIM = 64
V_DIM = 2 * HEAD_DIM
SSM_GROUP = 16
SSM_STATE = 64
CONV_W = 3
N_BRANCH = 3
REL_BUCKETS = 32
REL_MAX_DIST = 128
EPS = 1e-6
NEG = -1e30
LOG2E = 1.4426950408889634

V7X_VMEM_BYTES = 64 * 1024 * 1024
VMEM_LIMIT_BYTES = (V7X_VMEM_BYTES * 7) // 8
MXU_DIM = 256
SUBLANES = 8
ROW_TILE = 256
ATTN_TILE = 256
ATTN_HEADS_PER_STEP = 4
ONES_ROWS = 16
BIAS_FAR = 2
BIAS_MASKED = 3
SSM_CHUNK_ROWS = 256
PAGES_PER_STEP = 16
PAGE_GROUP = 2


def _cparams(*sem):
    return pltpu.CompilerParams(dimension_semantics=sem, vmem_limit_bytes=VMEM_LIMIT_BYTES)


def _resident():
    return pl.BlockSpec(memory_space=pltpu.VMEM)


def _rms(x, g):
    ms = jnp.mean(x * x, axis=-1, keepdims=True)
    return x * lax.rsqrt(ms + EPS) * g


def _dot(a, b):
    return jnp.dot(a, b, preferred_element_type=F32)


def _head_norm(acc, seg, gain):
    outs = []
    for c in range(acc.shape[1] // MXU_DIM):
        a = acc[:, c * MXU_DIM:(c + 1) * MXU_DIM]
        ms = _dot((a * a).astype(BF16), seg)
        outs.append(a * lax.rsqrt(ms + EPS))
    return jnp.concatenate(outs, axis=1) * gain


def _in_proj_kernel(*refs, tm, seq_len, d_model, conv_dim, carry_mode, fill_layers):
    n_state = 1 if carry_mode else 2
    n_in = 7 + n_state + (0 if fill_layers else 2)
    x_ref, gmix_ref, w_ref, qn_ref, kn_ref, seg_ref, wconv_ref = refs[:7]
    state_refs = refs[7:7 + n_state]
    q_ref, kf_ref, kb_ref, vf_ref, vb_ref, u_ref, z_ref, cbz_ref, g_ref = refs[n_in:n_in + 9]
    if carry_mode:
        (buf_ref,), carry_ref = state_refs, refs[n_in + 9]
    else:
        p1_ref, p2_ref = state_refs

    def store_layers(ref, val):
        if fill_layers:
            for layer in range(fill_layers):
                ref[layer] = val
        else:
            ref[...] = val
    d = d_model
    h = _rms(x_ref[...], gmix_ref[...]).astype(BF16)

    def proj(c0, c1):
        return _dot(h, w_ref[:, c0:c1])

    seg = seg_ref[...]
    q_ref[...] = _head_norm(proj(0, d), seg, qn_ref[...]).astype(BF16)
    k = _head_norm(proj(d, 2 * d), seg, kn_ref[...])
    store_layers(kf_ref, k)
    kb_ref[...] = k.astype(BF16)
    v = proj(2 * d, 3 * d)
    store_layers(vf_ref, v)
    if carry_mode:
        vt = v.T
        ones = jnp.ones((ONES_ROWS, tm), BF16)
        va = V_DIM + ONES_ROWS
        for hd in range(d // V_DIM):
            vb_ref[hd * va:hd * va + V_DIM, :] = vt[hd * V_DIM:(hd + 1) * V_DIM, :].astype(BF16)
            vb_ref[hd * va + V_DIM:(hd + 1) * va, :] = ones
    else:
        vb_ref[...] = v.astype(BF16)
    c0 = 3 * d
    ucb = proj(c0, c0 + 2 * conv_dim)
    u_ref[...] = ucb[:, :conv_dim]
    cb = ucb[:, conv_dim:]
    ccx = proj(c0 + 2 * conv_dim, c0 + 4 * conv_dim)
    z = ccx[:, :conv_dim] * ccx[:, conv_dim:]
    z_ref[...] = z

    row = lax.broadcasted_iota(jnp.int32, z.shape, 0)
    zr1 = pltpu.roll(z, 1, axis=0)
    zr2 = pltpu.roll(z, 2, axis=0)
    if carry_mode:
        tiles_per_seq = seq_len // tm

        @pl.when(pl.program_id(0) % tiles_per_seq == 0)
        def _():
            carry_ref[...] = buf_ref[...]

        c_m2 = carry_ref[SUBLANES - 2:SUBLANES - 1, :]
        c_m1 = carry_ref[SUBLANES - 1:SUBLANES, :]
        z1 = jnp.where(row == 0, c_m1, zr1)
        z2 = jnp.where(row == 0, c_m2, jnp.where(row == 1, c_m1, zr2))
        carry_ref[...] = z[tm - SUBLANES:tm, :]
    else:
        t = lax.rem(row, seq_len)
        z1 = jnp.where(t >= 1, zr1, p1_ref[...])
        z2 = jnp.where(t >= 2, zr2, p2_ref[...])
    zc = wconv_ref[0:1, :] * z2 + wconv_ref[1:2, :] * z1 + wconv_ref[2:3, :] * z
    cbz_ref[...] = (cb * zc).astype(BF16)

    g0 = c0 + 4 * conv_dim
    for c in range(N_BRANCH):
        g_ref[:, c * d:(c + 1) * d] = jax.nn.sigmoid(proj(g0 + c * d, g0 + (c + 1) * d))


def _in_proj(x, gmix, w_in, qn, kn, seg, wconv, conv_state, kv_stacks, *, seq_len, layer, depth):
    n, d = x.shape
    conv_dim = wconv.shape[1]
    tm = min(ROW_TILE, n)
    carry_mode = seq_len >= tm
    assert n % tm == 0 and seq_len >= CONV_W - 1
    assert (seq_len % tm == 0) if carry_mode else (tm % seq_len == 0)
    n_seq = n // seq_len
    row = lambda w: pl.BlockSpec((tm, w), lambda i: (i, 0))
    full = lambda a: pl.BlockSpec(a.shape, lambda i: (0,) * a.ndim)
    if carry_mode:
        buf = jnp.zeros((n_seq, SUBLANES, conv_dim), F32).at[:, SUBLANES - 2:, :].set(conv_state)
        tiles_per_seq = seq_len // tm
        state_args = (buf,)
        state_specs = [pl.BlockSpec((None, SUBLANES, conv_dim), lambda i: (i // tiles_per_seq, 0, 0))]
        scratch = [pltpu.VMEM((SUBLANES, conv_dim), F32)]
        vt_rows = (d // V_DIM) * (V_DIM + ONES_ROWS)
        v_att_shape = jax.ShapeDtypeStruct((n_seq, vt_rows, seq_len), BF16)
        v_att_spec = pl.BlockSpec((None, vt_rows, tm), lambda i: (i // tiles_per_seq, 0, i % tiles_per_seq))
    else:
        v_att_shape = jax.ShapeDtypeStruct((n, d), BF16)
        v_att_spec = row(d)
        zeros = jnp.zeros((n_seq, seq_len, conv_dim), F32)
        p1 = zeros.at[:, 0].set(conv_state[:, 1]).reshape(n, conv_dim)
        p2 = zeros.at[:, 0].set(conv_state[:, 0]).at[:, 1].set(conv_state[:, 1]).reshape(n, conv_dim)
        state_args = (p1, p2)
        state_specs = [row(conv_dim), row(conv_dim)]
        scratch = []
    out_shape = (
        jax.ShapeDtypeStruct((n, d), BF16),
        jax.ShapeDtypeStruct((depth, n, d), F32),
        jax.ShapeDtypeStruct((n, d), BF16),
        jax.ShapeDtypeStruct((depth, n, d), F32),
        v_att_shape,
        jax.ShapeDtypeStruct((n, conv_dim), F32),
        jax.ShapeDtypeStruct((n, conv_dim), F32),
        jax.ShapeDtypeStruct((n, conv_dim), BF16),
        jax.ShapeDtypeStruct((n, N_BRANCH * d), F32),
    )
    first = kv_stacks is None
    if first:
        kv_spec = pl.BlockSpec((depth, tm, d), lambda i: (0, i, 0))
        alias_args, alias_specs, aliases = (), [], {}
    else:
        kv_spec = pl.BlockSpec((None, tm, d), lambda i: (layer, i, 0))
        alias_args, alias_specs = tuple(kv_stacks), [pl.BlockSpec(memory_space=pl.ANY)] * 2
        n_in = 7 + len(state_args)
        aliases = {n_in: 1, n_in + 1: 3}
    out_specs = (row(d), kv_spec, row(d), kv_spec, v_att_spec, row(conv_dim), row(conv_dim),
                 row(conv_dim), row(N_BRANCH * d))
    return pl.pallas_call(
        functools.partial(_in_proj_kernel, tm=tm, seq_len=seq_len, d_model=d, conv_dim=conv_dim,
                          carry_mode=carry_mode, fill_layers=depth if first else 0),
        grid=(n // tm,),
        in_specs=[row(d), full(gmix), _resident(), full(qn), full(kn), full(seg), full(wconv)] + state_specs
        + alias_specs,
        input_output_aliases=aliases,
        out_specs=out_specs,
        out_shape=out_shape,
        scratch_shapes=scratch,
        compiler_params=_cparams("arbitrary"),
        name="in_proj",
    )(x, gmix, w_in, qn, kn, seg, wconv, *state_args, *alias_args)


def _rel_bias(n, tbl_ref, head):
    max_exact = REL_BUCKETS // 2
    nc = jnp.maximum(n, 0)
    nf = jnp.maximum(nc, max_exact).astype(F32)
    large = max_exact + (jnp.log(nf / max_exact) / math.log(REL_MAX_DIST / max_exact)
                         * (REL_BUCKETS - max_exact)).astype(jnp.int32)
    large = jnp.minimum(large, REL_BUCKETS - 1)
    bucket = jnp.where(nc < max_exact, nc, large)
    val = jnp.zeros(n.shape, F32)
    for b in range(REL_BUCKETS):
        val = jnp.where(bucket == b, tbl_ref[b, head], val)
    return jnp.where(n >= 0, val * LOG2E, NEG)


def _prompt_bias_kernel(tbl_ref, o_ref, *, tile):
    head, d = pl.program_id(0), pl.program_id(1)
    key = lax.broadcasted_iota(jnp.int32, (tile, tile), 0)
    qry = lax.broadcasted_iota(jnp.int32, (tile, tile), 1)
    n = jnp.where(d == BIAS_MASKED, -1, d * tile + qry - key)
    b = _rel_bias(n, tbl_ref, head)
    o_ref[...] = jnp.concatenate([b, b], axis=1)


def _prompt_bias(rel_table, tile):
    assert tile >= REL_MAX_DIST
    return pl.pallas_call(
        functools.partial(_prompt_bias_kernel, tile=tile),
        grid=(N_HEADS, BIAS_MASKED + 1),
        in_specs=[pl.BlockSpec(memory_space=pltpu.SMEM)],
        out_specs=pl.BlockSpec((None, None, tile, 2 * tile), lambda h, d: (h, d, 0, 0)),
        out_shape=jax.ShapeDtypeStruct((N_HEADS, BIAS_MASKED + 1, tile, 2 * tile), F32),
        compiler_params=_cparams("arbitrary", "arbitrary"),
        name="prompt_bias",
    )(rel_table)


def _sample_bias_kernel(tbl_ref, o_ref, onew_ref, *, past, t_new, page):
    shape = (N_HEADS * 2 * t_new, page * N_HEADS)
    r = lax.broadcasted_iota(jnp.int32, shape, 0)
    col = lax.broadcasted_iota(jnp.int32, shape, 1)
    first_key = jnp.where(pl.program_id(0) == 0, 0, past - page)
    n = past + lax.rem(r, t_new) - (first_key + col // N_HEADS)
    rn = lax.broadcasted_iota(jnp.int32, onew_ref.shape, 0)
    jn = lax.broadcasted_iota(jnp.int32, onew_ref.shape, 1)
    nn = jnp.where(jn < t_new, lax.rem(rn, t_new) - jn, -1)
    acc = jnp.full(shape, NEG, F32)
    accn = jnp.zeros(onew_ref.shape, F32)
    rows_per_head = 2 * t_new
    for head in range(N_HEADS):
        own = (r // rows_per_head == head) & (lax.rem(col, N_HEADS) == head)
        acc = jnp.where(own, _rel_bias(n, tbl_ref, head), acc)
        accn = jnp.where(rn // rows_per_head == head, _rel_bias(nn, tbl_ref, head), accn)
    o_ref[...] = acc
    onew_ref[...] = accn


def _sample_bias(rel_table, past, t_new, page):
    assert page >= REL_MAX_DIST and past >= 2 * page
    rows = N_HEADS * 2 * t_new
    return pl.pallas_call(
        functools.partial(_sample_bias_kernel, past=past, t_new=t_new, page=page),
        grid=(2,),
        in_specs=[pl.BlockSpec(memory_space=pltpu.SMEM)],
        out_specs=(pl.BlockSpec((None, rows, page * N_HEADS), lambda s: (s, 0, 0)),
                   pl.BlockSpec((rows, SUBLANES), lambda s: (0, 0))),
        out_shape=(jax.ShapeDtypeStruct((2, rows, page * N_HEADS), F32),
                   jax.ShapeDtypeStruct((rows, SUBLANES), F32)),
        compiler_params=_cparams("arbitrary"),
        name="sample_bias",
    )(rel_table)


def _lambda(lq1, lk1, lq2, lk2, lambda_init):
    s1 = jnp.sum(lq1[...] * lk1[...], axis=-1, keepdims=True)
    s2 = jnp.sum(lq2[...] * lk2[...], axis=-1, keepdims=True)
    return jnp.exp(s1) - jnp.exp(s2) + lambda_init


def _flash_kernel(q_ref, k_ref, vt_ref, bias_ref, sn_ref, lq1, lk1, lq2, lk2, o_ref, m_ref, acc_ref, s_ref,
                  *, tile, lambda_init):
    i = pl.program_id(2)
    heads = m_ref.shape[0]
    lane = lax.broadcasted_iota(jnp.int32, (tile, V_DIM), 1)
    zero = jnp.zeros((tile, V_DIM), BF16)
    q2 = []
    for hh in range(heads):
        q = q_ref[:, hh * V_DIM:(hh + 1) * V_DIM]
        q2.append(jnp.concatenate([jnp.where(lane < HEAD_DIM, q, zero), jnp.where(lane >= HEAD_DIM, q, zero)],
                                  axis=0))
    m_ref[...] = jnp.full(m_ref.shape, NEG, F32)
    acc_ref[...] = jnp.zeros(acc_ref.shape, F32)

    def key_start(j):
        return pl.multiple_of(jnp.minimum(j, i) * tile, tile)

    def scores(hh, j):
        return lax.dot_general(k_ref[pl.ds(key_start(j), tile), hh * V_DIM:(hh + 1) * V_DIM], q2[hh],
                               (((1,), (1,)), ((), ())), preferred_element_type=F32)

    def update(hh, j, s, bias, far):
        if bias is not None:
            s = s + bias
        smax = jnp.max(s, axis=0, keepdims=True)
        if far is not None:
            smax = smax + far
        m = m_ref[hh]
        m_new = jnp.maximum(m, smax)
        alpha = jnp.exp2(m - m_new)
        p = jnp.exp2(s - (m_new if far is None else m_new - far))
        acc_ref[hh] = alpha * acc_ref[hh] + _dot(vt_ref[hh, :, pl.ds(key_start(j), tile)], p.astype(BF16))
        m_ref[hh] = m_new

    n_pairs = lax.shift_right_logical(i + 2, 1)
    n_far_pairs = lax.shift_right_logical(jnp.maximum(i - 1, 0), 1)

    def pair(jj, far):
        for cur in (0, 1):
            j = 2 * jj + cur
            kind = jnp.where(j > i, BIAS_MASKED, jnp.minimum(i - j, BIAS_FAR))
            for hh in range(heads):
                s_ref[1 - cur, hh] = scores(hh, j + 1)
                if far:
                    update(hh, j, s_ref[cur, hh], None, bias_ref[hh, BIAS_FAR, 0:1, :])
                else:
                    update(hh, j, s_ref[cur, hh], bias_ref[hh, kind], None)

    for hh in range(heads):
        s_ref[0, hh] = scores(hh, 0)

    def far_body(jj, c):
        pair(jj, True)
        return c

    def near_body(jj, c):
        pair(jj, False)
        return c

    lax.fori_loop(0, n_far_pairs, far_body, 0)
    lax.fori_loop(n_far_pairs, n_pairs, near_body, 0)
    lam = _lambda(lq1, lk1, lq2, lk2, lambda_init)
    for hh in range(heads):
        o = acc_ref[hh, :V_DIM, :] * (1.0 / acc_ref[hh, V_DIM:V_DIM + 1, :])
        od = o[:, :tile] - lam * o[:, tile:]
        ms = jnp.mean(od * od, axis=0, keepdims=True)
        on = od * lax.rsqrt(ms + EPS) * sn_ref[...]
        o_ref[:, hh * V_DIM:(hh + 1) * V_DIM] = on.T.astype(BF16)


def _prompt_attention(q, k, vt, bias, sn, lq1, lk1, lq2, lk2, *, batch, seq_len, lambda_init):
    tile = ATTN_TILE
    hp = ATTN_HEADS_PER_STEP
    nq = seq_len // tile
    va = vt.shape[2]
    assert seq_len % tile == 0 and N_HEADS % hp == 0 and va == V_DIM + ONES_ROWS
    small = pl.BlockSpec((1, HEAD_DIM), lambda b, h, i: (0, 0))
    return pl.pallas_call(
        functools.partial(_flash_kernel, tile=tile, lambda_init=lambda_init),
        grid=(batch, N_HEADS // hp, nq),
        in_specs=[
            pl.BlockSpec((tile, hp * V_DIM), lambda b, h, i: (b * nq + i, h)),
            pl.BlockSpec((seq_len, hp * V_DIM), lambda b, h, i: (b, h)),
            pl.BlockSpec((None, hp, va, seq_len), lambda b, h, i: (b, h, 0, 0)),
            pl.BlockSpec((hp, BIAS_MASKED + 1, tile, 2 * tile), lambda b, h, i: (h, 0, 0, 0)),
            pl.BlockSpec((V_DIM, tile), lambda b, h, i: (0, 0)),
            small, small, small, small,
        ],
        out_specs=pl.BlockSpec((tile, hp * V_DIM), lambda b, h, i: (b * nq + i, h)),
        out_shape=jax.ShapeDtypeStruct(q.shape, BF16),
        scratch_shapes=[pltpu.VMEM((hp, 1, 2 * tile), F32), pltpu.VMEM((hp, va, 2 * tile), F32), pltpu.VMEM((2, hp, tile, 2 * tile), F32)],
        compiler_params=_cparams("arbitrary", "arbitrary", "arbitrary"),
        name="prompt_attention",
    )(q, k, vt, bias, sn, lq1, lk1, lq2, lk2)


def _paged_kernel(*refs, n_pages, t_new, lambda_init):
    k_refs = refs[1:1 + n_pages]
    v_refs = refs[1 + n_pages:1 + 2 * n_pages]
    (q_ref, bias_ref, kn_ref, vn_ref, bnew_ref, sn_ref, lq1, lk1, lq2, lk2,
     o_ref, m_ref, l_ref, acc_ref) = refs[1 + 2 * n_pages:]
    step = pl.program_id(1)
    rph = 2 * t_new
    rows = N_HEADS * rph
    cols = k_refs[0].shape[0]

    r = lax.broadcasted_iota(jnp.int32, (rows, V_DIM), 0)
    lane = lax.broadcasted_iota(jnp.int32, (rows, V_DIM), 1)
    own = (lane // HEAD_DIM) == lax.rem(r // t_new, 2)
    q = jnp.where(own, q_ref[...], 0.0)

    @pl.when(step == 0)
    def _():
        m_ref[...] = jnp.full(m_ref.shape, NEG, F32)
        l_ref[...] = jnp.zeros(l_ref.shape, F32)
        acc_ref[...] = jnp.zeros(acc_ref.shape, F32)

    def head_rows(x, h):
        return x[h * rph:(h + 1) * rph, :]

    def update(s, pv_fn):
        m = m_ref[...]
        m_new = jnp.maximum(m, jnp.max(s, axis=-1, keepdims=True))
        alpha = jnp.exp2(m - m_new)
        p = jnp.exp2(s - m_new)
        l_ref[...] = alpha * l_ref[...] + jnp.sum(p, axis=-1, keepdims=True)
        acc_ref[...] = alpha * acc_ref[...] + pv_fn(p)
        m_ref[...] = m_new

    def scores(qh, kb):
        return lax.dot_general(qh, kb, (((1,), (1,)), ((), ())), preferred_element_type=F32)

    last = step == pl.num_programs(1) - 1

    def group_scores(g):
        parts = []
        for c in range(g * PAGE_GROUP, (g + 1) * PAGE_GROUP):
            bias = bias_ref[0]
            if c == n_pages - 1:
                bias = jnp.where(last, bias_ref[1], bias)
            parts.append(scores(q, k_refs[c][...]) + bias)
        return jnp.concatenate(parts, axis=1)

    def group_pv(g):
        def pv(p):
            o = None
            for n, c in enumerate(range(g * PAGE_GROUP, (g + 1) * PAGE_GROUP)):
                t = _dot(p[:, n * cols:(n + 1) * cols], v_refs[c][...])
                o = t if o is None else o + t
            return o
        return pv

    n_groups = n_pages // PAGE_GROUP
    s = group_scores(0)
    for g in range(n_groups):
        s_next = group_scores(g + 1) if g + 1 < n_groups else None
        update(s, group_pv(g))
        s = s_next

    @pl.when(step == pl.num_programs(1) - 1)
    def _():
        lanes = lambda ref, h: ref[:, h * V_DIM:(h + 1) * V_DIM]
        s_new = jnp.concatenate([scores(head_rows(q, h), lanes(kn_ref, h)) for h in range(N_HEADS)], axis=0)
        update(s_new + bnew_ref[...],
               lambda p: jnp.concatenate([_dot(head_rows(p, h), lanes(vn_ref, h)) for h in range(N_HEADS)], axis=0))
        a = acc_ref[...] * (1.0 / l_ref[...])
        a2 = pltpu.roll(a, rows - t_new, axis=0)
        dm = a - _lambda(lq1, lk1, lq2, lk2, lambda_init) * a2
        ms = jnp.mean(dm * dm, axis=-1, keepdims=True)
        o_ref[...] = (dm * lax.rsqrt(ms + EPS) * sn_ref[...]).astype(BF16)


def _sample_attention(q_rows, cache_k, cache_v, page_table, bias, k_new, v_new, bias_new, sn,
                      lq1, lk1, lq2, lk2, *, layer, t_new, lambda_init):
    db, n_tbl = page_table.shape
    depth, n_pool, page, n_heads, v_dim = cache_k.shape
    d = n_heads * v_dim
    rows = N_HEADS * 2 * t_new
    npg = PAGES_PER_STEP
    assert n_tbl % npg == 0 and npg % PAGE_GROUP == 0 and 2 * t_new == SUBLANES
    assert n_heads == N_HEADS == SUBLANES and v_dim == V_DIM
    cache_k = cache_k.reshape(depth, n_pool, page * n_heads, v_dim)
    cache_v = cache_v.reshape(depth, n_pool, page * n_heads, v_dim)

    def page_spec(c):
        return pl.BlockSpec((None, None, page * n_heads, v_dim),
                            lambda b, s, pt: (layer, pt[b, s * npg + c], 0, 0))

    small = pl.BlockSpec((1, HEAD_DIM), lambda b, s, pt: (0, 0))
    in_specs = [page_spec(c) for c in range(npg)] * 2 + [
        pl.BlockSpec((None, rows, v_dim), lambda b, s, pt: (b, 0, 0)),
        pl.BlockSpec(bias.shape, lambda b, s, pt: (0, 0, 0)),
        pl.BlockSpec((None, SUBLANES, d), lambda b, s, pt: (b, 0, 0)),
        pl.BlockSpec((None, SUBLANES, d), lambda b, s, pt: (b, 0, 0)),
        pl.BlockSpec((rows, SUBLANES), lambda b, s, pt: (0, 0)),
        pl.BlockSpec((1, v_dim), lambda b, s, pt: (0, 0)),
        small, small, small, small,
    ]
    grid_spec = pltpu.PrefetchScalarGridSpec(
        num_scalar_prefetch=1,
        grid=(db, n_tbl // npg),
        in_specs=in_specs,
        out_specs=pl.BlockSpec((None, rows, v_dim), lambda b, s, pt: (b, 0, 0)),
        scratch_shapes=[pltpu.VMEM((rows, 1), F32), pltpu.VMEM((rows, 1), F32), pltpu.VMEM((rows, v_dim), F32)],
    )
    return pl.pallas_call(
        functools.partial(_paged_kernel, n_pages=npg, t_new=t_new, lambda_init=lambda_init),
        grid_spec=grid_spec,
        out_shape=jax.ShapeDtypeStruct((db, rows, v_dim), BF16),
        compiler_params=_cparams("arbitrary", "arbitrary"),
        name="sample_attention",
    )(page_table, *([cache_k] * npg), *([cache_v] * npg), q_rows, bias, k_new, v_new, bias_new, sn,
      lq1, lk1, lq2, lk2)


def _ssm_param_kernel(are_ref, aim_ref, ldt_ref, bre_ref, bim_ref, lre_ref, lim_ref, bbre_ref, bbim_ref,
                      lbre_ref, lbim_ref):
    ar, ai = are_ref[...], aim_ref[...]
    dt = jnp.exp(ldt_ref[...])
    mag = jnp.exp(ar * dt)
    lr = mag * jnp.cos(ai * dt)
    li = mag * jnp.sin(ai * dt)
    lre_ref[...] = lr
    lim_ref[...] = li
    nr, ni = lr - 1.0, li
    den = ar * ar + ai * ai
    fr = ((nr * ar + ni * ai) / den)[:, None, :]
    fi = ((ni * ar - nr * ai) / den)[:, None, :]
    br, bi = bre_ref[...], bim_ref[...]
    bbr = fr * br - fi * bi
    bbi = fr * bi + fi * br
    bbre_ref[...] = bbr
    bbim_ref[...] = bbi
    lr3, li3 = lr[:, None, :], li[:, None, :]
    lbre_ref[...] = lr3 * bbr - li3 * bbi
    lbim_ref[...] = lr3 * bbi + li3 * bbr


def _ssm_params(a_re, a_im, log_dt, b_re_t, b_im_t):
    g, p = a_re.shape
    c = b_re_t.shape[1]
    mat = jax.ShapeDtypeStruct((g, c, p), F32)
    return pl.pallas_call(
        _ssm_param_kernel,
        out_shape=(jax.ShapeDtypeStruct((g, p), F32), jax.ShapeDtypeStruct((g, p), F32), mat, mat, mat, mat),
        name="ssm_params",
    )(a_re, a_im, log_dt.reshape(g, 1), b_re_t, b_im_t)


def _ssm_kernel(u_ref, h0_ref, lre_ref, lim_ref, bre_ref, bim_ref, cre_ref, cim_ref, d_ref,
                y_ref, hfin_ref, hbuf, carry, *, batch, state_w):
    rc = u_ref.shape[0]
    sw = state_w

    @pl.when(pl.program_id(0) == 0)
    def _():
        carry[...] = h0_ref[...]

    rows = carry.shape[0]
    fold = batch != rows
    u = u_ref[...]
    ub = u.astype(BF16)
    n_in = bre_ref.shape[0]
    kin, nout = u.shape[1] // n_in, bre_ref.shape[2]
    if fold:
        assert 2 * batch == rows and bre_ref.shape[1] == 2 * kin
        upper_u = lax.rem(lax.broadcasted_iota(jnp.int32, u.shape, 0), rows) >= batch
        up = jnp.where(upper_u, pltpu.roll(u, batch, axis=0), 0.0).astype(BF16)
    for j in range(n_in):
        uj = ub[:, j * kin:(j + 1) * kin]
        if fold:
            uj = jnp.concatenate([uj, up[:, j * kin:(j + 1) * kin]], axis=1)
        hbuf[:, j * nout:(j + 1) * nout] = _dot(uj, bre_ref[j])
        hbuf[:, sw + j * nout:sw + (j + 1) * nout] = _dot(uj, bim_ref[j])

    lr = jnp.broadcast_to(lre_ref[...], (rows, sw))
    li = jnp.broadcast_to(lim_ref[...], (rows, sw))
    if fold:
        upper = lax.broadcasted_iota(jnp.int32, (rows, sw), 0) >= batch
        lr, li = jnp.where(upper, lr * lr - li * li, lr), jnp.where(upper, 2.0 * lr * li, li)

    def body(t, c):
        cr, ci = c
        r0 = pl.multiple_of(t * rows, rows)
        hr = lr * cr - li * ci + hbuf[pl.ds(r0, rows), :sw]
        hi = lr * ci + li * cr + hbuf[pl.ds(r0, rows), sw:]
        hbuf[pl.ds(r0, rows), :sw] = hr
        hbuf[pl.ds(r0, rows), sw:] = hi
        if fold:
            hr = jnp.where(upper, hr, pltpu.roll(hr, batch, axis=0))
            hi = jnp.where(upper, hi, pltpu.roll(hi, batch, axis=0))
        return hr, hi

    cr, ci = lax.fori_loop(0, rc // rows, body, (carry[:, :sw], carry[:, sw:]))
    carry[:, :sw] = cr
    carry[:, sw:] = ci
    hfin_ref[:, :sw] = cr
    hfin_ref[:, sw:] = ci

    n_out = cre_ref.shape[0]
    kout, wout = cre_ref.shape[1], cre_ref.shape[2]
    ys = []
    for o in range(n_out):
        hr = hbuf[:, o * kout:(o + 1) * kout].astype(BF16)
        hi = hbuf[:, sw + o * kout:sw + (o + 1) * kout].astype(BF16)
        ys.append(_dot(hr, cre_ref[o]) + _dot(hi, cim_ref[o]))
    y = jnp.concatenate(ys, axis=1) + d_ref[...] * u
    y_ref[...] = jax.nn.gelu(y).astype(BF16)
    del wout


def _ssm(u_tb, h0, lre, lim, bre, bim, cre, cim, dskip, *, batch):
    n, ch = u_tb.shape
    sw = lre.shape[1]
    rows = h0.shape[0]
    rc = min(SSM_CHUNK_ROWS, n)
    assert n % rc == 0 and rc % rows == 0
    full = lambda a: pl.BlockSpec(a.shape, lambda i: (0,) * a.ndim)
    return pl.pallas_call(
        functools.partial(_ssm_kernel, batch=batch, state_w=sw),
        grid=(n // rc,),
        in_specs=[pl.BlockSpec((rc, ch), lambda i: (i, 0)), full(h0), full(lre), full(lim),
                  full(bre), full(bim), full(cre), full(cim), full(dskip)],
        out_specs=(pl.BlockSpec((rc, ch), lambda i: (i, 0)), full(h0)),
        out_shape=(jax.ShapeDtypeStruct((n, ch), BF16), jax.ShapeDtypeStruct(h0.shape, F32)),
        scratch_shapes=[pltpu.VMEM((rc, 2 * sw), F32), pltpu.VMEM(h0.shape, F32)],
        compiler_params=_cparams("arbitrary"),
        name="ssm",
    )(u_tb, h0, lre, lim, bre, bim, cre, cim, dskip)


def _post_kernel(x_ref, o_ref, ys_ref, cbz_ref, g_ref, pe_ref, gffn_ref, gple_ref,
                 wao_ref, wga_ref, wgb_ref, wco_ref, wo_ref, wg_ref, wu_ref, wd_ref, wpg_ref, wpp_ref, out_ref):
    d = x_ref.shape[1]
    ys = ys_ref[...]
    attn_out = _dot(o_ref[...], wao_ref[...])
    ssm_out = _dot(ys, wga_ref[...]) * jax.nn.sigmoid(_dot(ys, wgb_ref[...]))
    conv_out = _dot(cbz_ref[...], wco_ref[...])
    merged = (g_ref[:, 0:d] * attn_out + g_ref[:, d:2 * d] * ssm_out + g_ref[:, 2 * d:3 * d] * conv_out)
    x = x_ref[...] + _dot(merged.astype(BF16), wo_ref[...])
    h = _rms(x, gffn_ref[...]).astype(BF16)
    act = jax.nn.silu(_dot(h, wg_ref[...])) * _dot(h, wu_ref[...])
    x = x + _dot(act.astype(BF16), wd_ref[...])
    h = _rms(x, gple_ref[...]).astype(BF16)
    gate = jax.nn.sigmoid(_dot(h, wpg_ref[...]))
    out_ref[...] = x + gate * _dot(pe_ref[...].astype(BF16), wpp_ref[...])


def _post(x, o, ys, cbz, g, pe, gffn, gple, weights):
    n, d = x.shape
    tm = min(ROW_TILE, n)
    row = lambda a: pl.BlockSpec((tm, a.shape[1]), lambda i: (i, 0))
    full = lambda a: pl.BlockSpec(a.shape, lambda i: (0,) * a.ndim)
    return pl.pallas_call(
        _post_kernel,
        grid=(n // tm,),
        in_specs=[row(x), row(o), row(ys), row(cbz), row(g), row(pe), full(gffn), full(gple)]
        + [_resident()] * len(weights),
        out_specs=row(x),
        out_shape=jax.ShapeDtypeStruct((n, d), F32),
        compiler_params=_cparams("parallel"),
        name="post",
    )(x, o, ys, cbz, g, pe, gffn, gple, *weights)


def _block_diag(blocks):
    n, r, c = blocks.shape
    eye = jnp.eye(n, dtype=bool)
    return jnp.where(eye[:, None, :, None], blocks[:, :, None, :], 0).reshape(n * r, n * c)


def _ssm_matrices(lw):
    g, p = lw["a_re"].shape
    lre, lim, bbre, bbim, lbre, lbim = _ssm_params(
        lw["a_re"], lw["a_im"], lw["log_dt"], jnp.swapaxes(lw["b_re"], 1, 2), jnp.swapaxes(lw["b_im"], 1, 2))
    c = bbre.shape[1]
    gin = 128 // c
    b_in = lambda bb: jax.vmap(_block_diag)(bb.reshape(g // gin, gin, c, p)).astype(BF16)
    b_fold = lambda bb, lb: jnp.concatenate([b_in(bb), b_in(lb)], axis=1)
    gout = MXU_DIM // c
    c_out = lambda cc: jax.vmap(_block_diag)(
        jnp.swapaxes(cc, 1, 2).reshape(g // gout, gout, p, c)).astype(BF16)
    return dict(lam=(lre.reshape(1, g * p), lim.reshape(1, g * p)),
                b_plain=(b_in(bbre), b_in(bbim)), b_fold=(b_fold(bbre, lbre), b_fold(bbim, lbim)),
                c=(c_out(lw["c_re"]), c_out(-lw["c_im"])))


def _layer(x, pe, attend, h0_re, h0_im, conv_state, kv_stacks, lw, mats, lambda_init, *, batch, seq_len, layer,
           depth):
    n, d = x.shape
    outs = _in_proj(x, lw["g_mix"], lw["w_in"], lw["qn"], lw["kn"], lw["seg"], lw["w_conv"], conv_state,
                    kv_stacks, seq_len=seq_len, layer=layer, depth=depth)
    q, k_f, k_b, v_f, v_b, u, z, cbz, gates = outs
    o = attend(q, k_b, v_b)

    lre, lim = mats["lam"]
    cre, cim = mats["c"]
    sw = lre.shape[1]
    ch = u.shape[1]
    u_tb = u.reshape(batch, seq_len, ch).swapaxes(0, 1).reshape(n, ch)
    h0 = jnp.concatenate([h0_re.reshape(batch, sw), h0_im.reshape(batch, sw)], axis=1)
    rows = max(batch, SUBLANES)
    if batch < rows:
        assert rows == 2 * batch and seq_len % 2 == 0
        h0 = jnp.concatenate([h0, h0], axis=0)
        bre, bim = mats["b_fold"]
    else:
        bre, bim = mats["b_plain"]
    ys_tb, hfin = _ssm(u_tb, h0, lre, lim, bre, bim, cre, cim, lw["d_skip"], batch=batch)
    ys = ys_tb.reshape(seq_len, batch, ch).swapaxes(0, 1).reshape(n, ch)
    hfin = hfin[rows - batch:]
    groups = sw // SSM_STATE

    weights = tuple(lw[k] for k in ("w_attn_out", "w_glu_a", "w_glu_b", "w_conv_out", "w_o", "w_ffn_gate",
                                    "w_ffn_up", "w_ffn_down", "w_ple_gate", "w_ple_proj"))
    x2 = _post(x, o, ys, cbz, gates, pe, lw["g_ffn"], lw["g_ple"], weights)
    return (x2, (k_f, v_f),
            (hfin[:, :sw].reshape(batch, groups, SSM_STATE),
             hfin[:, sw:].reshape(batch, groups, SSM_STATE),
             z.reshape(batch, seq_len, ch)[:, seq_len - (CONV_W - 1):]))


def kernel(x_prompt, x_sample, cache_k, cache_v, state_ssm_re, state_ssm_im, state_conv, page_table, p_prompt, p_sample, rel_table, g_mix, w_in, q_norm, k_norm, lambda_q1, lambda_k1, lambda_q2, lambda_k2, sub_norm, w_attn_out, ssm_a_re, ssm_a_im, ssm_log_dt, ssm_b_re, ssm_b_im, ssm_c_re, ssm_c_im, ssm_d, w_glu_a, w_glu_b, w_conv, w_conv_out, w_o, g_ffn, w_ffn_gate, w_ffn_up, w_ffn_down, g_ple, w_ple_gate, w_ple_proj):
    bp, seq, d = x_prompt.shape
    db, t_new, _ = x_sample.shape
    depth = w_in.shape[0]
    page = cache_k.shape[2]
    past = page_table.shape[1] * page
    conv_dim = w_conv.shape[2]

    bias_p = _prompt_bias(rel_table, ATTN_TILE)
    bias_s, bias_new = _sample_bias(rel_table, past, t_new, page)
    seg = (_block_diag(jnp.ones((MXU_DIM // HEAD_DIM, HEAD_DIM, HEAD_DIM), F32)) / HEAD_DIM).astype(BF16)
    reps = d // HEAD_DIM

    xp = x_prompt.reshape(bp * seq, d)
    xs = x_sample.reshape(db * t_new, d)
    outs_p, outs_s = [], []
    kv_p = kv_s = None
    for i in range(depth):
        lambda_init = 0.8 - 0.6 * math.exp(-0.3 * i)
        bf = lambda w: w[i].astype(BF16)
        vec = lambda w: w[i].reshape(1, -1)
        lw = dict(
            g_mix=vec(g_mix), w_in=bf(w_in), w_conv=w_conv[i], seg=seg,
            qn=jnp.tile(q_norm[i], reps).reshape(1, d) * (HEAD_DIM ** -0.5 * LOG2E),
            kn=jnp.tile(k_norm[i], reps).reshape(1, d),
            w_attn_out=bf(w_attn_out), w_glu_a=bf(w_glu_a), w_glu_b=bf(w_glu_b),
            w_conv_out=bf(w_conv_out), w_o=bf(w_o), g_ffn=vec(g_ffn), w_ffn_gate=bf(w_ffn_gate),
            w_ffn_up=bf(w_ffn_up), w_ffn_down=bf(w_ffn_down), g_ple=vec(g_ple),
            w_ple_gate=bf(w_ple_gate), w_ple_proj=bf(w_ple_proj),
            a_re=ssm_a_re[i], a_im=ssm_a_im[i], log_dt=ssm_log_dt[i], b_re=ssm_b_re[i], b_im=ssm_b_im[i],
            c_re=ssm_c_re[i], c_im=ssm_c_im[i], d_skip=vec(ssm_d),
        )
        mats = _ssm_matrices(lw)
        lam_args = tuple(w[i].reshape(1, HEAD_DIM) for w in (lambda_q1, lambda_k1, lambda_q2, lambda_k2))
        sn_scaled = sub_norm[i] * (1.0 - lambda_init)

        def attend_prompt(q, k_b, v_b):
            vt = v_b.reshape(bp, N_HEADS, V_DIM + ONES_ROWS, seq)
            sn = jnp.broadcast_to(sn_scaled[:, None], (V_DIM, ATTN_TILE))
            return _prompt_attention(q, k_b, vt, bias_p, sn, *lam_args, batch=bp, seq_len=seq,
                                     lambda_init=lambda_init)

        def attend_sample(q, k_b, v_b):
            qh = q.astype(F32).reshape(db, t_new, N_HEADS, V_DIM).transpose(0, 2, 1, 3)
            q_rows = jnp.broadcast_to(qh[:, :, None], (db, N_HEADS, 2, t_new, V_DIM)).reshape(db, -1, V_DIM)
            pad = lambda a: jnp.pad(a.astype(F32).reshape(db, t_new, d), ((0, 0), (0, SUBLANES - t_new), (0, 0)))
            o = _sample_attention(q_rows, cache_k, cache_v, page_table, bias_s, pad(k_b), pad(v_b),
                                  bias_new, sn_scaled.reshape(1, V_DIM), *lam_args, layer=i, t_new=t_new,
                                  lambda_init=lambda_init)
            o = o.reshape(db, N_HEADS, 2, t_new, V_DIM)[:, :, 0]
            return o.transpose(0, 2, 1, 3).reshape(db * t_new, d)

        zs = jnp.zeros((bp, d // 2 // SSM_GROUP * SSM_STATE), F32)
        xp, kv_p, small_p = _layer(xp, p_prompt[i].reshape(bp * seq, -1), attend_prompt, zs, zs,
                                   jnp.zeros((bp, CONV_W - 1, conv_dim), F32), kv_p, lw, mats, lambda_init,
                                   batch=bp, seq_len=seq, layer=i, depth=depth)
        xs, kv_s, small_s = _layer(xs, p_sample[i].reshape(db * t_new, -1), attend_sample, state_ssm_re[i],
                                   state_ssm_im[i], state_conv[i], kv_s, lw, mats, lambda_init,
                                   batch=db, seq_len=t_new, layer=i, depth=depth)
        outs_p.append(small_p)
        outs_s.append(small_s)

    stack = lambda outs, j: jnp.stack([o[j] for o in outs])
    heads = lambda kv, b, t: tuple(a.reshape(depth, b, t, N_HEADS, V_DIM) for a in kv)
    return ((xp.reshape(bp, seq, d), xs.reshape(db, t_new, d))
            + heads(kv_p, bp, seq) + tuple(stack(outs_p, j) for j in range(3))
            + heads(kv_s, db, t_new) + tuple(stack(outs_s, j) for j in range(3)))
```

```python
import functools
import math

import jax
import jax.numpy as jnp
from jax import lax
from jax.experimental import pallas as pl
from jax.experimental.pallas import tpu as pltpu

F32 = jnp.float32
BF16 = jnp.bfloat16

N_HEADS = 8
HEAD_DIM = 64
V_DIM = 2 * HEAD_DIM
SSM_GROUP = 16
SSM_STATE = 64
CONV_W = 3
N_BRANCH = 3
REL_BUCKETS = 32
REL_MAX_DIST = 128
EPS = 1e-6
NEG = -1e30
LOG2E = 1.4426950408889634

V7X_VMEM_BYTES = 64 * 1024 * 1024
VMEM_LIMIT_BYTES = (V7X_VMEM_BYTES * 7) // 8
MXU_DIM = 256
SUBLANES = 8
LANES = 128
ROW_TILE = 256
ATTN_TILE = 256
ATTN_HEADS_PER_STEP = 4
ONES_ROWS = 16
BIAS_FAR = 2
BIAS_MASKED = 3
SSM_CHUNK_ROWS = 256
PAGES_PER_STEP = 16
PAGE_GROUP = 2


def _cparams(*sem):
    return pltpu.CompilerParams(dimension_semantics=sem, vmem_limit_bytes=VMEM_LIMIT_BYTES)


def _resident():
    return pl.BlockSpec(memory_space=pltpu.VMEM)


def _rms(x, g):
    ms = jnp.mean(x * x, axis=-1, keepdims=True)
    return x * lax.rsqrt(ms + EPS) * g


def _dot(a, b):
    return jnp.dot(a, b, preferred_element_type=F32)


def _head_norm(acc, seg, gain):
    outs = []
    for c in range(acc.shape[1] // MXU_DIM):
        a = acc[:, c * MXU_DIM:(c + 1) * MXU_DIM]
        ms = _dot((a * a).astype(BF16), seg)
        outs.append(a * lax.rsqrt(ms + EPS))
    return jnp.concatenate(outs, axis=1) * gain


def _in_proj_kernel(*refs, tm, seq_len, d_model, conv_dim, carry_mode, fill_layers):
    n_state = 1 if carry_mode else 2
    n_in = 7 + n_state + (0 if fill_layers else 2)
    x_ref, gmix_ref, w_ref, qn_ref, kn_ref, seg_ref, wconv_ref = refs[:7]
    state_refs = refs[7:7 + n_state]
    q_ref, kf_ref, kb_ref, vf_ref, vb_ref, u_ref, z_ref, cbz_ref, g_ref = refs[n_in:n_in + 9]
    if carry_mode:
        (buf_ref,), carry_ref = state_refs, refs[n_in + 9]
    else:
        p1_ref, p2_ref = state_refs

    def store_layers(ref, val):
        if fill_layers:
            for layer in range(fill_layers):
                ref[layer] = val
        else:
            ref[...] = val
    d = d_model
    h = _rms(x_ref[...], gmix_ref[...]).astype(BF16)

    def proj(c0, c1):
        return _dot(h, w_ref[:, c0:c1])

    seg = seg_ref[...]
    q_ref[...] = _head_norm(proj(0, d), seg, qn_ref[...]).astype(BF16)
    k = _head_norm(proj(d, 2 * d), seg, kn_ref[...])
    store_layers(kf_ref, k)
    kb_ref[...] = k.astype(BF16)
    v = proj(2 * d, 3 * d)
    store_layers(vf_ref, v)
    if carry_mode:
        vt = v.T
        ones = jnp.ones((ONES_ROWS, tm), BF16)
        va = V_DIM + ONES_ROWS
        for hd in range(d // V_DIM):
            vb_ref[hd * va:hd * va + V_DIM, :] = vt[hd * V_DIM:(hd + 1) * V_DIM, :].astype(BF16)
            vb_ref[hd * va + V_DIM:(hd + 1) * va, :] = ones
    else:
        vb_ref[...] = v.astype(BF16)
    c0 = 3 * d
    ucb = proj(c0, c0 + 2 * conv_dim)
    u_ref[...] = ucb[:, :conv_dim]
    cb = ucb[:, conv_dim:]
    ccx = proj(c0 + 2 * conv_dim, c0 + 4 * conv_dim)
    z = ccx[:, :conv_dim] * ccx[:, conv_dim:]
    z_ref[...] = z

    row = lax.broadcasted_iota(jnp.int32, z.shape, 0)
    zr1 = pltpu.roll(z, 1, axis=0)
    zr2 = pltpu.roll(z, 2, axis=0)
    if carry_mode:
        tiles_per_seq = seq_len // tm

        @pl.when(pl.program_id(0) % tiles_per_seq == 0)
        def _():
            carry_ref[...] = buf_ref[...]

        c_m2 = carry_ref[SUBLANES - 2:SUBLANES - 1, :]
        c_m1 = carry_ref[SUBLANES - 1:SUBLANES, :]
        z1 = jnp.where(row == 0, c_m1, zr1)
        z2 = jnp.where(row == 0, c_m2, jnp.where(row == 1, c_m1, zr2))
        carry_ref[...] = z[tm - SUBLANES:tm, :]
    else:
        t = lax.rem(row, seq_len)
        z1 = jnp.where(t >= 1, zr1, p1_ref[...])
        z2 = jnp.where(t >= 2, zr2, p2_ref[...])
    zc = wconv_ref[0:1, :] * z2 + wconv_ref[1:2, :] * z1 + wconv_ref[2:3, :] * z
    cbz_ref[...] = (cb * zc).astype(BF16)

    g0 = c0 + 4 * conv_dim
    for c in range(N_BRANCH):
        g_ref[:, c * d:(c + 1) * d] = jax.nn.sigmoid(proj(g0 + c * d, g0 + (c + 1) * d))


def _in_proj(x, gmix, w_in, qn, kn, seg, wconv, conv_state, kv_stacks, *, seq_len, layer, depth):
    n, d = x.shape
    conv_dim = wconv.shape[1]
    tm = min(ROW_TILE, n)
    carry_mode = seq_len >= tm
    assert n % tm == 0 and seq_len >= CONV_W - 1
    assert (seq_len % tm == 0) if carry_mode else (tm % seq_len == 0)
    n_seq = n // seq_len
    row = lambda w: pl.BlockSpec((tm, w), lambda i: (i, 0))
    full = lambda a: pl.BlockSpec(a.shape, lambda i: (0,) * a.ndim)
    if carry_mode:
        buf = jnp.zeros((n_seq, SUBLANES, conv_dim), F32).at[:, SUBLANES - 2:, :].set(conv_state)
        tiles_per_seq = seq_len // tm
        state_args = (buf,)
        state_specs = [pl.BlockSpec((None, SUBLANES, conv_dim), lambda i: (i // tiles_per_seq, 0, 0))]
        scratch = [pltpu.VMEM((SUBLANES, conv_dim), F32)]
        vt_rows = (d // V_DIM) * (V_DIM + ONES_ROWS)
        v_att_shape = jax.ShapeDtypeStruct((n_seq, vt_rows, seq_len), BF16)
        v_att_spec = pl.BlockSpec((None, vt_rows, tm), lambda i: (i // tiles_per_seq, 0, i % tiles_per_seq))
    else:
        v_att_shape = jax.ShapeDtypeStruct((n, d), BF16)
        v_att_spec = row(d)
        zeros = jnp.zeros((n_seq, seq_len, conv_dim), F32)
        p1 = zeros.at[:, 0].set(conv_state[:, 1]).reshape(n, conv_dim)
        p2 = zeros.at[:, 0].set(conv_state[:, 0]).at[:, 1].set(conv_state[:, 1]).reshape(n, conv_dim)
        state_args = (p1, p2)
        state_specs = [row(conv_dim), row(conv_dim)]
        scratch = []
    out_shape = (
        jax.ShapeDtypeStruct((n, d), BF16),
        jax.ShapeDtypeStruct((depth, n, d), F32),
        jax.ShapeDtypeStruct((n, d), BF16),
        jax.ShapeDtypeStruct((depth, n, d), F32),
        v_att_shape,
        jax.ShapeDtypeStruct((n, conv_dim), F32),
        jax.ShapeDtypeStruct((n, conv_dim), F32),
        jax.ShapeDtypeStruct((n, conv_dim), BF16),
        jax.ShapeDtypeStruct((n, N_BRANCH * d), F32),
    )
    first = kv_stacks is None
    if first:
        kv_spec = pl.BlockSpec((depth, tm, d), lambda i: (0, i, 0))
        alias_args, alias_specs, aliases = (), [], {}
    else:
        kv_spec = pl.BlockSpec((None, tm, d), lambda i: (layer, i, 0))
        alias_args, alias_specs = tuple(kv_stacks), [pl.BlockSpec(memory_space=pl.ANY)] * 2
        n_in = 7 + len(state_args)
        aliases = {n_in: 1, n_in + 1: 3}
    out_specs = (row(d), kv_spec, row(d), kv_spec, v_att_spec, row(conv_dim), row(conv_dim),
                 row(conv_dim), row(N_BRANCH * d))
    return pl.pallas_call(
        functools.partial(_in_proj_kernel, tm=tm, seq_len=seq_len, d_model=d, conv_dim=conv_dim,
                          carry_mode=carry_mode, fill_layers=depth if first else 0),
        grid=(n // tm,),
        in_specs=[row(d), full(gmix), _resident(), full(qn), full(kn), full(seg), full(wconv)] + state_specs
        + alias_specs,
        input_output_aliases=aliases,
        out_specs=out_specs,
        out_shape=out_shape,
        scratch_shapes=scratch,
        compiler_params=_cparams("arbitrary"),
        name="in_proj",
    )(x, gmix, w_in, qn, kn, seg, wconv, *state_args, *alias_args)


def _rel_bias(n, tbl_ref, head):
    max_exact = REL_BUCKETS // 2
    nc = jnp.maximum(n, 0)
    nf = jnp.maximum(nc, max_exact).astype(F32)
    large = max_exact + (jnp.log(nf / max_exact) / math.log(REL_MAX_DIST / max_exact)
                         * (REL_BUCKETS - max_exact)).astype(jnp.int32)
    large = jnp.minimum(large, REL_BUCKETS - 1)
    bucket = jnp.where(nc < max_exact, nc, large)
    val = jnp.zeros(n.shape, F32)
    for b in range(REL_BUCKETS):
        val = jnp.where(bucket == b, tbl_ref[b, head], val)
    return jnp.where(n >= 0, val * LOG2E, NEG)


def _prompt_bias_kernel(tbl_ref, o_ref, *, tile):
    head, d = pl.program_id(0), pl.program_id(1)
    key = lax.broadcasted_iota(jnp.int32, (tile, tile), 0)
    qry = lax.broadcasted_iota(jnp.int32, (tile, tile), 1)
    n = jnp.where(d == BIAS_MASKED, -1, d * tile + qry - key)
    b = _rel_bias(n, tbl_ref, head)
    o_ref[...] = jnp.concatenate([b, b], axis=1)


def _prompt_bias(rel_table, tile):
    assert tile >= REL_MAX_DIST
    return pl.pallas_call(
        functools.partial(_prompt_bias_kernel, tile=tile),
        grid=(N_HEADS, BIAS_MASKED + 1),
        in_specs=[pl.BlockSpec(memory_space=pltpu.SMEM)],
        out_specs=pl.BlockSpec((None, None, tile, 2 * tile), lambda h, d: (h, d, 0, 0)),
        out_shape=jax.ShapeDtypeStruct((N_HEADS, BIAS_MASKED + 1, tile, 2 * tile), F32),
        compiler_params=_cparams("arbitrary", "arbitrary"),
        name="prompt_bias",
    )(rel_table)


def _sample_bias_kernel(tbl_ref, o_ref, onew_ref, *, past, t_new, page):
    shape = (N_HEADS * 2 * t_new, page * N_HEADS)
    r = lax.broadcasted_iota(jnp.int32, shape, 0)
    col = lax.broadcasted_iota(jnp.int32, shape, 1)
    first_key = jnp.where(pl.program_id(0) == 0, 0, past - page)
    n = past + lax.rem(r, t_new) - (first_key + col // N_HEADS)
    rn = lax.broadcasted_iota(jnp.int32, onew_ref.shape, 0)
    jn = lax.broadcasted_iota(jnp.int32, onew_ref.shape, 1)
    nn = jnp.where(jn < t_new, lax.rem(rn, t_new) - jn, -1)
    acc = jnp.full(shape, NEG, F32)
    accn = jnp.zeros(onew_ref.shape, F32)
    rows_per_head = 2 * t_new
    for head in range(N_HEADS):
        own = (r // rows_per_head == head) & (lax.rem(col, N_HEADS) == head)
        acc = jnp.where(own, _rel_bias(n, tbl_ref, head), acc)
        accn = jnp.where(rn // rows_per_head == head, _rel_bias(nn, tbl_ref, head), accn)
    o_ref[...] = acc
    onew_ref[...] = accn


def _sample_bias(rel_table, past, t_new, page):
    assert page >= REL_MAX_DIST and past >= 2 * page
    rows = N_HEADS * 2 * t_new
    return pl.pallas_call(
        functools.partial(_sample_bias_kernel, past=past, t_new=t_new, page=page),
        grid=(2,),
        in_specs=[pl.BlockSpec(memory_space=pltpu.SMEM)],
        out_specs=(pl.BlockSpec((None, rows, page * N_HEADS), lambda s: (s, 0, 0)),
                   pl.BlockSpec((rows, SUBLANES), lambda s: (0, 0))),
        out_shape=(jax.ShapeDtypeStruct((2, rows, page * N_HEADS), F32),
                   jax.ShapeDtypeStruct((rows, SUBLANES), F32)),
        compiler_params=_cparams("arbitrary"),
        name="sample_bias",
    )(rel_table)


def _lambda(lq1, lk1, lq2, lk2, lambda_init):
    s1 = jnp.sum(lq1[...] * lk1[...], axis=-1, keepdims=True)
    s2 = jnp.sum(lq2[...] * lk2[...], axis=-1, keepdims=True)
    return jnp.exp(s1) - jnp.exp(s2) + lambda_init


def _flash_kernel(q_ref, k_ref, vt_ref, bias_ref, sn_ref, lq1, lk1, lq2, lk2, o_ref, m_ref, acc_ref, s_ref,
                  *, tile, lambda_init):
    i = pl.program_id(2)
    heads = m_ref.shape[0]
    lane = lax.broadcasted_iota(jnp.int32, (tile, V_DIM), 1)
    zero = jnp.zeros((tile, V_DIM), BF16)
    q2 = []
    for hh in range(heads):
        q = q_ref[:, hh * V_DIM:(hh + 1) * V_DIM]
        q2.append(jnp.concatenate([jnp.where(lane < HEAD_DIM, q, zero), jnp.where(lane >= HEAD_DIM, q, zero)],
                                  axis=0))
    m_ref[...] = jnp.full(m_ref.shape, NEG, F32)
    acc_ref[...] = jnp.zeros(acc_ref.shape, F32)

    def key_start(j):
        return pl.multiple_of(jnp.minimum(j, i) * tile, tile)

    def scores(hh, j):
        return lax.dot_general(k_ref[pl.ds(key_start(j), tile), hh * V_DIM:(hh + 1) * V_DIM], q2[hh],
                               (((1,), (1,)), ((), ())), preferred_element_type=F32)

    def update(hh, j, s, bias, far):
        if bias is not None:
            s = s + bias
        smax = jnp.max(s, axis=0, keepdims=True)
        if far is not None:
            smax = smax + far
        m = m_ref[hh]
        m_new = jnp.maximum(m, smax)
        alpha = jnp.exp2(m - m_new)
        p = jnp.exp2(s - (m_new if far is None else m_new - far))
        acc_ref[hh] = alpha * acc_ref[hh] + _dot(vt_ref[hh, :, pl.ds(key_start(j), tile)], p.astype(BF16))
        m_ref[hh] = m_new

    n_pairs = lax.shift_right_logical(i + 2, 1)
    n_far_pairs = lax.shift_right_logical(jnp.maximum(i - 1, 0), 1)

    def pair(jj, far):
        for cur in (0, 1):
            j = 2 * jj + cur
            kind = jnp.where(j > i, BIAS_MASKED, jnp.minimum(i - j, BIAS_FAR))
            for hh in range(heads):
                s_ref[1 - cur, hh] = scores(hh, j + 1)
                if far:
                    update(hh, j, s_ref[cur, hh], None, bias_ref[hh, BIAS_FAR, 0:1, :])
                else:
                    update(hh, j, s_ref[cur, hh], bias_ref[hh, kind], None)

    for hh in range(heads):
        s_ref[0, hh] = scores(hh, 0)

    def far_body(jj, c):
        pair(jj, True)
        return c

    def near_body(jj, c):
        pair(jj, False)
        return c

    lax.fori_loop(0, n_far_pairs, far_body, 0)
    lax.fori_loop(n_far_pairs, n_pairs, near_body, 0)
    lam = _lambda(lq1, lk1, lq2, lk2, lambda_init)
    for hh in range(heads):
        o = acc_ref[hh, :V_DIM, :] * (1.0 / acc_ref[hh, V_DIM:V_DIM + 1, :])
        od = o[:, :tile] - lam * o[:, tile:]
        ms = jnp.mean(od * od, axis=0, keepdims=True)
        on = od * lax.rsqrt(ms + EPS) * sn_ref[...]
        o_ref[:, hh * V_DIM:(hh + 1) * V_DIM] = on.T.astype(BF16)


def _prompt_attention(q, k, vt, bias, sn, lq1, lk1, lq2, lk2, *, batch, seq_len, lambda_init):
    tile = ATTN_TILE
    hp = ATTN_HEADS_PER_STEP
    nq = seq_len // tile
    va = vt.shape[2]
    assert seq_len % tile == 0 and N_HEADS % hp == 0 and va == V_DIM + ONES_ROWS
    small = pl.BlockSpec((1, HEAD_DIM), lambda b, h, i: (0, 0))
    return pl.pallas_call(
        functools.partial(_flash_kernel, tile=tile, lambda_init=lambda_init),
        grid=(batch, N_HEADS // hp, nq),
        in_specs=[
            pl.BlockSpec((tile, hp * V_DIM), lambda b, h, i: (b * nq + i, h)),
            pl.BlockSpec((seq_len, hp * V_DIM), lambda b, h, i: (b, h)),
            pl.BlockSpec((None, hp, va, seq_len), lambda b, h, i: (b, h, 0, 0)),
            pl.BlockSpec((hp, BIAS_MASKED + 1, tile, 2 * tile), lambda b, h, i: (h, 0, 0, 0)),
            pl.BlockSpec((V_DIM, tile), lambda b, h, i: (0, 0)),
            small, small, small, small,
        ],
        out_specs=pl.BlockSpec((tile, hp * V_DIM), lambda b, h, i: (b * nq + i, h)),
        out_shape=jax.ShapeDtypeStruct(q.shape, BF16),
        scratch_shapes=[pltpu.VMEM((hp, 1, 2 * tile), F32), pltpu.VMEM((hp, va, 2 * tile), F32), pltpu.VMEM((2, hp, tile, 2 * tile), F32)],
        compiler_params=_cparams("arbitrary", "arbitrary", "arbitrary"),
        name="prompt_attention",
    )(q, k, vt, bias, sn, lq1, lk1, lq2, lk2)


def _paged_kernel(*refs, n_pages, t_new, lambda_init):
    k_refs = refs[1:1 + n_pages]
    v_refs = refs[1 + n_pages:1 + 2 * n_pages]
    (q_ref, bias_ref, kn_ref, vn_ref, bnew_ref, sn_ref, lq1, lk1, lq2, lk2,
     o_ref, m_ref, l_ref, acc_ref) = refs[1 + 2 * n_pages:]
    step = pl.program_id(1)
    rph = 2 * t_new
    rows = N_HEADS * rph
    cols = k_refs[0].shape[0]

    r = lax.broadcasted_iota(jnp.int32, (rows, V_DIM), 0)
    lane = lax.broadcasted_iota(jnp.int32, (rows, V_DIM), 1)
    own = (lane // HEAD_DIM) == lax.rem(r // t_new, 2)
    q = jnp.where(own, q_ref[...], 0.0)

    @pl.when(step == 0)
    def _():
        m_ref[...] = jnp.full(m_ref.shape, NEG, F32)
        l_ref[...] = jnp.zeros(l_ref.shape, F32)
        acc_ref[...] = jnp.zeros(acc_ref.shape, F32)

    def head_rows(x, h):
        return x[h * rph:(h + 1) * rph, :]

    def update(s, pv_fn):
        m = m_ref[...]
        m_new = jnp.maximum(m, jnp.max(s, axis=-1, keepdims=True))
        alpha = jnp.exp2(m - m_new)
        p = jnp.exp2(s - m_new)
        l_ref[...] = alpha * l_ref[...] + jnp.sum(p, axis=-1, keepdims=True)
        acc_ref[...] = alpha * acc_ref[...] + pv_fn(p)
        m_ref[...] = m_new

    def scores(qh, kb):
        return lax.dot_general(qh, kb, (((1,), (1,)), ((), ())), preferred_element_type=F32)

    last = step == pl.num_programs(1) - 1

    def group_scores(g):
        parts = []
        for c in range(g * PAGE_GROUP, (g + 1) * PAGE_GROUP):
            bias = bias_ref[0]
            if c == n_pages - 1:
                bias = jnp.where(last, bias_ref[1], bias)
            parts.append(scores(q, k_refs[c][...]) + bias)
        return jnp.concatenate(parts, axis=1)

    def group_pv(g):
        def pv(p):
            o = None
            for n, c in enumerate(range(g * PAGE_GROUP, (g + 1) * PAGE_GROUP)):
                t = _dot(p[:, n * cols:(n + 1) * cols], v_refs[c][...])
                o = t if o is None else o + t
            return o
        return pv

    n_groups = n_pages // PAGE_GROUP
    s = group_scores(0)
    for g in range(n_groups):
        s_next = group_scores(g + 1) if g + 1 < n_groups else None
        update(s, group_pv(g))
        s = s_next

    @pl.when(step == pl.num_programs(1) - 1)
    def _():
        lanes = lambda ref, h: ref[:, h * V_DIM:(h + 1) * V_DIM]
        s_new = jnp.concatenate([scores(head_rows(q, h), lanes(kn_ref, h)) for h in range(N_HEADS)], axis=0)
        update(s_new + bnew_ref[...],
               lambda p: jnp.concatenate([_dot(head_rows(p, h), lanes(vn_ref, h)) for h in range(N_HEADS)], axis=0))
        a = acc_ref[...] * (1.0 / l_ref[...])
        a2 = pltpu.roll(a, rows - t_new, axis=0)
        dm = a - _lambda(lq1, lk1, lq2, lk2, lambda_init) * a2
        ms = jnp.mean(dm * dm, axis=-1, keepdims=True)
        o_ref[...] = (dm * lax.rsqrt(ms + EPS) * sn_ref[...]).astype(BF16)


def _sample_attention(q_rows, cache_k, cache_v, page_table, bias, k_new, v_new, bias_new, sn,
                      lq1, lk1, lq2, lk2, *, layer, t_new, lambda_init):
    db, n_tbl = page_table.shape
    depth, n_pool, page, n_heads, v_dim = cache_k.shape
    d = n_heads * v_dim
    rows = N_HEADS * 2 * t_new
    npg = PAGES_PER_STEP
    assert n_tbl % npg == 0 and npg % PAGE_GROUP == 0 and 2 * t_new == SUBLANES
    assert n_heads == N_HEADS == SUBLANES and v_dim == V_DIM
    cache_k = cache_k.reshape(depth, n_pool, page * n_heads, v_dim)
    cache_v = cache_v.reshape(depth, n_pool, page * n_heads, v_dim)

    def page_spec(c):
        return pl.BlockSpec((None, None, page * n_heads, v_dim),
                            lambda b, s, pt: (layer, pt[b, s * npg + c], 0, 0))

    small = pl.BlockSpec((1, HEAD_DIM), lambda b, s, pt: (0, 0))
    in_specs = [page_spec(c) for c in range(npg)] * 2 + [
        pl.BlockSpec((None, rows, v_dim), lambda b, s, pt: (b, 0, 0)),
        pl.BlockSpec(bias.shape, lambda b, s, pt: (0, 0, 0)),
        pl.BlockSpec((None, SUBLANES, d), lambda b, s, pt: (b, 0, 0)),
        pl.BlockSpec((None, SUBLANES, d), lambda b, s, pt: (b, 0, 0)),
        pl.BlockSpec((rows, SUBLANES), lambda b, s, pt: (0, 0)),
        pl.BlockSpec((1, v_dim), lambda b, s, pt: (0, 0)),
        small, small, small, small,
    ]
    grid_spec = pltpu.PrefetchScalarGridSpec(
        num_scalar_prefetch=1,
        grid=(db, n_tbl // npg),
        in_specs=in_specs,
        out_specs=pl.BlockSpec((None, rows, v_dim), lambda b, s, pt: (b, 0, 0)),
        scratch_shapes=[pltpu.VMEM((rows, 1), F32), pltpu.VMEM((rows, 1), F32), pltpu.VMEM((rows, v_dim), F32)],
    )
    return pl.pallas_call(
        functools.partial(_paged_kernel, n_pages=npg, t_new=t_new, lambda_init=lambda_init),
        grid_spec=grid_spec,
        out_shape=jax.ShapeDtypeStruct((db, rows, v_dim), BF16),
        compiler_params=_cparams("arbitrary", "arbitrary"),
        name="sample_attention",
    )(page_table, *([cache_k] * npg), *([cache_v] * npg), q_rows, bias, k_new, v_new, bias_new, sn,
      lq1, lk1, lq2, lk2)


def _ssm_param_kernel(are_ref, aim_ref, ldt_ref, bre_ref, bim_ref, lre_ref, lim_ref, bbre_ref, bbim_ref,
                      lbre_ref, lbim_ref):
    ar, ai = are_ref[...], aim_ref[...]
    dt = jnp.exp(ldt_ref[...])
    mag = jnp.exp(ar * dt)
    lr = mag * jnp.cos(ai * dt)
    li = mag * jnp.sin(ai * dt)
    lre_ref[...] = lr
    lim_ref[...] = li
    nr, ni = lr - 1.0, li
    den = ar * ar + ai * ai
    fr = ((nr * ar + ni * ai) / den)[:, None, :]
    fi = ((ni * ar - nr * ai) / den)[:, None, :]
    br, bi = bre_ref[...], bim_ref[...]
    bbr = fr * br - fi * bi
    bbi = fr * bi + fi * br
    bbre_ref[...] = bbr
    bbim_ref[...] = bbi
    lr3, li3 = lr[:, None, :], li[:, None, :]
    lbre_ref[...] = lr3 * bbr - li3 * bbi
    lbim_ref[...] = lr3 * bbi + li3 * bbr


def _ssm_params(a_re, a_im, log_dt, b_re_t, b_im_t):
    g, p = a_re.shape
    c = b_re_t.shape[1]
    mat = jax.ShapeDtypeStruct((g, c, p), F32)
    return pl.pallas_call(
        _ssm_param_kernel,
        out_shape=(jax.ShapeDtypeStruct((g, p), F32), jax.ShapeDtypeStruct((g, p), F32), mat, mat, mat, mat),
        name="ssm_params",
    )(a_re, a_im, log_dt.reshape(g, 1), b_re_t, b_im_t)


def _ssm_kernel(u_ref, h0_ref, lre_ref, lim_ref, bre_ref, bim_ref, cre_ref, cim_ref, d_ref,
                y_ref, hfin_ref, hbuf, carry, u_tb, y_tb, *, batch, state_w):
    steps = u_ref.shape[1]
    rc = steps * batch
    sw = state_w

    @pl.when(pl.program_id(0) == 0)
    def _():
        carry[...] = h0_ref[...]

    n_lane = u_tb.shape[0]
    lanes = u_tb.shape[2]
    for b in range(batch):
        for c in range(n_lane):
            u_tb[c, pl.ds(b, steps, stride=batch), :] = u_ref[b, :, c * lanes:(c + 1) * lanes]
    rows = carry.shape[0]
    fold = batch != rows
    u = jnp.concatenate([u_tb[c] for c in range(n_lane)], axis=1)
    ub = u.astype(BF16)
    n_in = bre_ref.shape[0]
    kin, nout = u.shape[1] // n_in, bre_ref.shape[2]
    if fold:
        assert 2 * batch == rows and bre_ref.shape[1] == 2 * kin
        upper_u = lax.rem(lax.broadcasted_iota(jnp.int32, u.shape, 0), rows) >= batch
        up = jnp.where(upper_u, pltpu.roll(u, batch, axis=0), 0.0).astype(BF16)
    for j in range(n_in):
        uj = ub[:, j * kin:(j + 1) * kin]
        if fold:
            uj = jnp.concatenate([uj, up[:, j * kin:(j + 1) * kin]], axis=1)
        hbuf[:, j * nout:(j + 1) * nout] = _dot(uj, bre_ref[j])
        hbuf[:, sw + j * nout:sw + (j + 1) * nout] = _dot(uj, bim_ref[j])

    lr = jnp.broadcast_to(lre_ref[...], (rows, sw))
    li = jnp.broadcast_to(lim_ref[...], (rows, sw))
    if fold:
        upper = lax.broadcasted_iota(jnp.int32, (rows, sw), 0) >= batch
        lr, li = jnp.where(upper, lr * lr - li * li, lr), jnp.where(upper, 2.0 * lr * li, li)

    def body(t, c):
        cr, ci = c
        r0 = pl.multiple_of(t * rows, rows)
        hr = lr * cr - li * ci + hbuf[pl.ds(r0, rows), :sw]
        hi = lr * ci + li * cr + hbuf[pl.ds(r0, rows), sw:]
        hbuf[pl.ds(r0, rows), :sw] = hr
        hbuf[pl.ds(r0, rows), sw:] = hi
        if fold:
            hr = jnp.where(upper, hr, pltpu.roll(hr, batch, axis=0))
            hi = jnp.where(upper, hi, pltpu.roll(hi, batch, axis=0))
        return hr, hi

    cr, ci = lax.fori_loop(0, rc // rows, body, (carry[:, :sw], carry[:, sw:]))
    carry[:, :sw] = cr
    carry[:, sw:] = ci
    hfin_ref[:, :sw] = cr
    hfin_ref[:, sw:] = ci

    n_out = cre_ref.shape[0]
    kout, wout = cre_ref.shape[1], cre_ref.shape[2]
    ys = []
    for o in range(n_out):
        hr = hbuf[:, o * kout:(o + 1) * kout].astype(BF16)
        hi = hbuf[:, sw + o * kout:sw + (o + 1) * kout].astype(BF16)
        ys.append(_dot(hr, cre_ref[o]) + _dot(hi, cim_ref[o]))
    y = jnp.concatenate(ys, axis=1) + d_ref[...] * u
    y = jax.nn.gelu(y)
    for c in range(n_lane):
        y_tb[c] = y[:, c * lanes:(c + 1) * lanes]
    for b in range(batch):
        y_ref[b] = jnp.concatenate([y_tb[c, pl.ds(b, steps, stride=batch), :] for c in range(n_lane)],
                                   axis=1).astype(BF16)
    del wout


def _ssm(u, h0, lre, lim, bre, bim, cre, cim, dskip):
    batch, seq_len, ch = u.shape
    sw = lre.shape[1]
    rows = h0.shape[0]
    steps = min(SSM_CHUNK_ROWS // batch, seq_len)
    rc = steps * batch
    assert seq_len % steps == 0 and rc % rows == 0
    full = lambda a: pl.BlockSpec(a.shape, lambda i: (0,) * a.ndim)
    chunk = pl.BlockSpec((batch, steps, ch), lambda i: (0, i, 0))
    return pl.pallas_call(
        functools.partial(_ssm_kernel, batch=batch, state_w=sw),
        grid=(seq_len // steps,),
        in_specs=[chunk, full(h0), full(lre), full(lim), full(bre), full(bim), full(cre), full(cim), full(dskip)],
        out_specs=(chunk, full(h0)),
        out_shape=(jax.ShapeDtypeStruct(u.shape, BF16), jax.ShapeDtypeStruct(h0.shape, F32)),
        scratch_shapes=[pltpu.VMEM((rc, 2 * sw), F32), pltpu.VMEM(h0.shape, F32),
                        pltpu.VMEM((ch // LANES, rc, LANES), F32), pltpu.VMEM((ch // LANES, rc, LANES), F32)],
        compiler_params=_cparams("arbitrary"),
        name="ssm",
    )(u, h0, lre, lim, bre, bim, cre, cim, dskip)


def _post_kernel(x_ref, o_ref, ys_ref, cbz_ref, g_ref, pe_ref, gffn_ref, gple_ref,
                 wao_ref, wga_ref, wgb_ref, wco_ref, wo_ref, wg_ref, wu_ref, wd_ref, wpg_ref, wpp_ref, out_ref):
    d = x_ref.shape[1]
    ys = ys_ref[...]
    attn_out = _dot(o_ref[...], wao_ref[...])
    ssm_out = _dot(ys, wga_ref[...]) * jax.nn.sigmoid(_dot(ys, wgb_ref[...]))
    conv_out = _dot(cbz_ref[...], wco_ref[...])
    merged = (g_ref[:, 0:d] * attn_out + g_ref[:, d:2 * d] * ssm_out + g_ref[:, 2 * d:3 * d] * conv_out)
    x = x_ref[...] + _dot(merged.astype(BF16), wo_ref[...])
    h = _rms(x, gffn_ref[...]).astype(BF16)
    act = jax.nn.silu(_dot(h, wg_ref[...])) * _dot(h, wu_ref[...])
    x = x + _dot(act.astype(BF16), wd_ref[...])
    h = _rms(x, gple_ref[...]).astype(BF16)
    gate = jax.nn.sigmoid(_dot(h, wpg_ref[...]))
    out_ref[...] = x + gate * _dot(pe_ref[...].astype(BF16), wpp_ref[...])


def _post(x, o, ys, cbz, g, pe, gffn, gple, weights):
    n, d = x.shape
    tm = min(ROW_TILE, n)
    row = lambda a: pl.BlockSpec((tm, a.shape[1]), lambda i: (i, 0))
    full = lambda a: pl.BlockSpec(a.shape, lambda i: (0,) * a.ndim)
    return pl.pallas_call(
        _post_kernel,
        grid=(n // tm,),
        in_specs=[row(x), row(o), row(ys), row(cbz), row(g), row(pe), full(gffn), full(gple)]
        + [_resident()] * len(weights),
        out_specs=row(x),
        out_shape=jax.ShapeDtypeStruct((n, d), F32),
        compiler_params=_cparams("parallel"),
        name="post",
    )(x, o, ys, cbz, g, pe, gffn, gple, *weights)


def _block_diag(blocks):
    n, r, c = blocks.shape
    eye = jnp.eye(n, dtype=bool)
    return jnp.where(eye[:, None, :, None], blocks[:, :, None, :], 0).reshape(n * r, n * c)


def _ssm_matrices(lw):
    g, p = lw["a_re"].shape
    lre, lim, bbre, bbim, lbre, lbim = _ssm_params(
        lw["a_re"], lw["a_im"], lw["log_dt"], jnp.swapaxes(lw["b_re"], 1, 2), jnp.swapaxes(lw["b_im"], 1, 2))
    c = bbre.shape[1]
    gin = 128 // c
    b_in = lambda bb: jax.vmap(_block_diag)(bb.reshape(g // gin, gin, c, p)).astype(BF16)
    b_fold = lambda bb, lb: jnp.concatenate([b_in(bb), b_in(lb)], axis=1)
    gout = MXU_DIM // c
    c_out = lambda cc: jax.vmap(_block_diag)(
        jnp.swapaxes(cc, 1, 2).reshape(g // gout, gout, p, c)).astype(BF16)
    return dict(lam=(lre.reshape(1, g * p), lim.reshape(1, g * p)),
                b_plain=(b_in(bbre), b_in(bbim)), b_fold=(b_fold(bbre, lbre), b_fold(bbim, lbim)),
                c=(c_out(lw["c_re"]), c_out(-lw["c_im"])))


def _layer(x, pe, attend, h0_re, h0_im, conv_state, kv_stacks, lw, mats, lambda_init, *, batch, seq_len, layer,
           depth):
    n, d = x.shape
    outs = _in_proj(x, lw["g_mix"], lw["w_in"], lw["qn"], lw["kn"], lw["seg"], lw["w_conv"], conv_state,
                    kv_stacks, seq_len=seq_len, layer=layer, depth=depth)
    q, k_f, k_b, v_f, v_b, u, z, cbz, gates = outs
    o = attend(q, k_b, v_b)

    lre, lim = mats["lam"]
    cre, cim = mats["c"]
    sw = lre.shape[1]
    ch = u.shape[1]
    h0 = jnp.concatenate([h0_re.reshape(batch, sw), h0_im.reshape(batch, sw)], axis=1)
    rows = max(batch, SUBLANES)
    if batch < rows:
        assert rows == 2 * batch and seq_len % 2 == 0
        h0 = jnp.concatenate([h0, h0], axis=0)
        bre, bim = mats["b_fold"]
    else:
        bre, bim = mats["b_plain"]
    ys, hfin = _ssm(u.reshape(batch, seq_len, ch), h0, lre, lim, bre, bim, cre, cim, lw["d_skip"])
    ys = ys.reshape(n, ch)
    hfin = hfin[rows - batch:]
    groups = sw // SSM_STATE

    weights = tuple(lw[k] for k in ("w_attn_out", "w_glu_a", "w_glu_b", "w_conv_out", "w_o", "w_ffn_gate",
                                    "w_ffn_up", "w_ffn_down", "w_ple_gate", "w_ple_proj"))
    x2 = _post(x, o, ys, cbz, gates, pe, lw["g_ffn"], lw["g_ple"], weights)
    return (x2, (k_f, v_f),
            (hfin[:, :sw].reshape(batch, groups, SSM_STATE),
             hfin[:, sw:].reshape(batch, groups, SSM_STATE),
             z.reshape(batch, seq_len, ch)[:, seq_len - (CONV_W - 1):]))


def kernel(x_prompt, x_sample, cache_k, cache_v, state_ssm_re, state_ssm_im, state_conv, page_table, p_prompt, p_sample, rel_table, g_mix, w_in, q_norm, k_norm, lambda_q1, lambda_k1, lambda_q2, lambda_k2, sub_norm, w_attn_out, ssm_a_re, ssm_a_im, ssm_log_dt, ssm_b_re, ssm_b_im, ssm_c_re, ssm_c_im, ssm_d, w_glu_a, w_glu_b, w_conv, w_conv_out, w_o, g_ffn, w_ffn_gate, w_ffn_up, w_ffn_down, g_ple, w_ple_gate, w_ple_proj):
    bp, seq, d = x_prompt.shape
    db, t_new, _ = x_sample.shape
    depth = w_in.shape[0]
    page = cache_k.shape[2]
    past = page_table.shape[1] * page
    conv_dim = w_conv.shape[2]

    bias_p = _prompt_bias(rel_table, ATTN_TILE)
    bias_s, bias_new = _sample_bias(rel_table, past, t_new, page)
    seg = (_block_diag(jnp.ones((MXU_DIM // HEAD_DIM, HEAD_DIM, HEAD_DIM), F32)) / HEAD_DIM).astype(BF16)
    reps = d // HEAD_DIM

    xp = x_prompt.reshape(bp * seq, d)
    xs = x_sample.reshape(db * t_new, d)
    outs_p, outs_s = [], []
    kv_p = kv_s = None
    for i in range(depth):
        lambda_init = 0.8 - 0.6 * math.exp(-0.3 * i)
        bf = lambda w: w[i].astype(BF16)
        vec = lambda w: w[i].reshape(1, -1)
        lw = dict(
            g_mix=vec(g_mix), w_in=bf(w_in), w_conv=w_conv[i], seg=seg,
            qn=jnp.tile(q_norm[i], reps).reshape(1, d) * (HEAD_DIM ** -0.5 * LOG2E),
            kn=jnp.tile(k_norm[i], reps).reshape(1, d),
            w_attn_out=bf(w_attn_out), w_glu_a=bf(w_glu_a), w_glu_b=bf(w_glu_b),
            w_conv_out=bf(w_conv_out), w_o=bf(w_o), g_ffn=vec(g_ffn), w_ffn_gate=bf(w_ffn_gate),
            w_ffn_up=bf(w_ffn_up), w_ffn_down=bf(w_ffn_down), g_ple=vec(g_ple),
            w_ple_gate=bf(w_ple_gate), w_ple_proj=bf(w_ple_proj),
            a_re=ssm_a_re[i], a_im=ssm_a_im[i], log_dt=ssm_log_dt[i], b_re=ssm_b_re[i], b_im=ssm_b_im[i],
            c_re=ssm_c_re[i], c_im=ssm_c_im[i], d_skip=vec(ssm_d),
        )
        mats = _ssm_matrices(lw)
        lam_args = tuple(w[i].reshape(1, HEAD_DIM) for w in (lambda_q1, lambda_k1, lambda_q2, lambda_k2))
        sn_scaled = sub_norm[i] * (1.0 - lambda_init)

        def attend_prompt(q, k_b, v_b):
            vt = v_b.reshape(bp, N_HEADS, V_DIM + ONES_ROWS, seq)
            sn = jnp.broadcast_to(sn_scaled[:, None], (V_DIM, ATTN_TILE))
            return _prompt_attention(q, k_b, vt, bias_p, sn, *lam_args, batch=bp, seq_len=seq,
                                     lambda_init=lambda_init)

        def attend_sample(q, k_b, v_b):
            qh = q.astype(F32).reshape(db, t_new, N_HEADS, V_DIM).transpose(0, 2, 1, 3)
            q_rows = jnp.broadcast_to(qh[:, :, None], (db, N_HEADS, 2, t_new, V_DIM)).reshape(db, -1, V_DIM)
            pad = lambda a: jnp.pad(a.astype(F32).reshape(db, t_new, d), ((0, 0), (0, SUBLANES - t_new), (0, 0)))
            o = _sample_attention(q_rows, cache_k, cache_v, page_table, bias_s, pad(k_b), pad(v_b),
                                  bias_new, sn_scaled.reshape(1, V_DIM), *lam_args, layer=i, t_new=t_new,
                                  lambda_init=lambda_init)
            o = o.reshape(db, N_HEADS, 2, t_new, V_DIM)[:, :, 0]
            return o.transpose(0, 2, 1, 3).reshape(db * t_new, d)

        zs = jnp.zeros((bp, d // 2 // SSM_GROUP * SSM_STATE), F32)
        xp, kv_p, small_p = _layer(xp, p_prompt[i].reshape(bp * seq, -1), attend_prompt, zs, zs,
                                   jnp.zeros((bp, CONV_W - 1, conv_dim), F32), kv_p, lw, mats, lambda_init,
                                   batch=bp, seq_len=seq, layer=i, depth=depth)
        xs, kv_s, small_s = _layer(xs, p_sample[i].reshape(db * t_new, -1), attend_sample, state_ssm_re[i],
                                   state_ssm_im[i], state_conv[i], kv_s, lw, mats, lambda_init,
                                   batch=db, seq_len=t_new, layer=i, depth=depth)
        outs_p.append(small_p)
        outs_s.append(small_s)

    stack = lambda outs, j: jnp.stack([o[j] for o in outs])
    heads = lambda kv, b, t: tuple(a.reshape(depth, b, t, N_HEADS, V_DIM) for a in kv)
    return ((xp.reshape(bp, seq, d), xs.reshape(db, t_new, d))
            + heads(kv_p, bp, seq) + tuple(stack(outs_p, j) for j in range(3))
            + heads(kv_s, db, t_new) + tuple(stack(outs_s, j) for j in range(3)))
```
